```python
import math
import jax, jax.numpy as jnp
from jax import lax
import numpy as np

D_MODEL = 2048
BATCH = 4
SEQ = 2048
DEPTH = 2
DEC_BATCH = 32
DEC_SEQ = 4
PAST_LEN = 16384
PAGE_SIZE = 128

HEAD_DIM = 128
A_HEADS = D_MODEL // 256
A_KV = A_HEADS // 4
A_WINDOW = 128
B_GROUPS = ((128, 1), (512, 4), (2048, 16))
N_BGROUPS = 3
B_HPG = D_MODEL // 256
B_KV = B_HPG // 4
C_HEADS = D_MODEL // 256
C_WIDTH = C_HEADS * HEAD_DIM
CONV_K = 4
CHUNK = 64
BAND_BLOCK = 128
D_FF = ((8 * D_MODEL // 3 + 255) // 256) * 256
N_EXPERTS = 8
TOP_K = 2
D_FF_EXPERT = 7 * D_MODEL // 2
N_DENSE = (DEPTH + 1) // 2
N_MOE = DEPTH // 2
PROJ_SIZES = (A_HEADS * HEAD_DIM, A_KV * HEAD_DIM, A_KV * HEAD_DIM,
              N_BGROUPS * B_HPG * HEAD_DIM, N_BGROUPS * B_KV * HEAD_DIM, N_BGROUPS * B_KV * HEAD_DIM,
              3 * C_WIDTH, C_WIDTH, C_HEADS, C_HEADS, 3 * D_MODEL)
PROJ_WIDTH = sum(PROJ_SIZES)
EPS = 1e-6

kernel_name = 'hybrid_gated_branch_decoder_step'


def rmsnorm(x, g):
    xf = x.astype(jnp.float32)
    y = xf * lax.rsqrt(jnp.mean(xf * xf, axis=-1, keepdims=True) + EPS) * g.astype(jnp.float32)
    return y.astype(x.dtype)


def l2norm(x):
    return x * lax.rsqrt(jnp.sum(x * x, axis=-1, keepdims=True) + EPS)


def alibi_slopes(n):
    return jnp.asarray(2.0 ** (-8.0 * np.arange(1, n + 1) / n), dtype=jnp.float32)


def last_rows(x, n):
    L = x.shape[1]
    if L < n:
        x = jnp.pad(x, [(0, 0), (n - L, 0)] + [(0, 0)] * (x.ndim - 2))
    return x[:, x.shape[1] - n:]


def to_stride(x, d):
    n, L = x.shape[:2]
    return jnp.swapaxes(x.reshape(n, L // d, d, *x.shape[2:]), 1, 2).reshape(n * d, L // d, *x.shape[2:])


def from_stride(x, d, n):
    ld = x.shape[1]
    return jnp.swapaxes(x.reshape(n, d, ld, *x.shape[2:]), 1, 2).reshape(n, ld * d, *x.shape[2:])


def banded_attention(q, k, v, slopes, step, max_units, sink=None):
    f32 = jnp.float32
    n, L, H, Dh = q.shape
    G = k.shape[2]
    R = H // G
    nb = -(-L // BAND_BLOCK)
    Lp = nb * BAND_BLOCK
    pad = ((0, 0), (0, Lp - L), (0, 0), (0, 0))
    qf = jnp.pad(q.astype(f32), pad).reshape(n, nb, BAND_BLOCK, G, R, Dh) * (Dh ** -0.5)

    def with_prev(x):
        xb = jnp.pad(x.astype(f32), pad).reshape(n, nb, BAND_BLOCK, G, Dh)
        prev = jnp.pad(xb[:, :-1], ((0, 0), (1, 0), (0, 0), (0, 0), (0, 0)))
        return jnp.concatenate([prev, xb], axis=2)

    kc, vc = with_prev(k), with_prev(v)
    s = jnp.einsum('nbqgrd,nbkgd->nbgrqk', qf, kc)
    ki = jnp.arange(2 * BAND_BLOCK)
    dist = (jnp.arange(BAND_BLOCK)[:, None] + BAND_BLOCK) - ki[None, :]
    key_unit = (jnp.arange(nb)[:, None, None] - 1) * BAND_BLOCK + ki[None, None, :]
    mask = (dist >= 0) & (dist <= max_units) & (key_unit >= 0)
    pen = (slopes.astype(f32).reshape(G, R) * step)[:, :, None, None] * dist.astype(f32)
    s = jnp.where(mask[:, None, None], s - pen, -jnp.inf)
    lse = jax.nn.logsumexp(s, axis=-1)
    if sink is not None:
        lse = jnp.logaddexp(lse, sink.astype(f32).reshape(G, R)[:, :, None])
    p = jnp.exp(s - lse[..., None])
    o = jnp.einsum('nbgrqk,nbkgd->nbqgrd', p, vc).reshape(n, Lp, H, Dh)[:, :L]
    lse = jnp.transpose(lse, (0, 1, 4, 2, 3)).reshape(n, Lp, H)[:, :L]
    return o, lse


def window_decode(q, buf_k, buf_v, k_new, v_new, slopes, dil, sink=None):
    f32 = jnp.float32
    b, T, H, Dh = q.shape
    W, G = buf_k.shape[1], buf_k.shape[2]
    R = H // G
    nk = W // dil + 1
    ctx_k = jnp.concatenate([buf_k, k_new.astype(buf_k.dtype)], axis=1)
    ctx_v = jnp.concatenate([buf_v, v_new.astype(buf_v.dtype)], axis=1)
    j = jnp.arange(T)[:, None]
    i = jnp.arange(nk)[None, :]
    idx = W + j - i * dil
    valid = (PAST_LEN + j - i * dil) >= 0
    kg = ctx_k[:, idx].astype(f32)
    vg = ctx_v[:, idx].astype(f32)
    qf = q.astype(f32).reshape(b, T, G, R, Dh) * (Dh ** -0.5)
    s = jnp.einsum('btgrd,btkgd->btgrk', qf, kg)
    pen = (slopes.astype(f32).reshape(G, R) * dil)[:, :, None] * i.astype(f32)
    s = jnp.where(valid[:, None, None, :], s - pen, -jnp.inf)
    lse = jax.nn.logsumexp(s, axis=-1)
    if sink is not None:
        lse = jnp.logaddexp(lse, sink.astype(f32).reshape(G, R))
    p = jnp.exp(s - lse[..., None])
    o = jnp.einsum('btgrk,btkgd->btgrd', p, vg).reshape(b, T, H, Dh)
    return o, lse.reshape(b, T, H), ctx_k[:, T:], ctx_v[:, T:]


def combine_groups(outs, lses):
    wts = jax.nn.softmax(jnp.stack(lses), axis=0)
    return jnp.einsum('gnlh,gnlhd->nlhd', wts, jnp.stack(outs))


def gated_delta_chunked(q, k, v, g, beta, S0):
    n, L, H, Dk = q.shape
    nc = L // CHUNK

    def chunks(x):
        return jnp.swapaxes(x.reshape(n, nc, CHUNK, H, *x.shape[3:]), 2, 3)

    qc = chunks(q) * (Dk ** -0.5)
    kc, vc, bc = chunks(k), chunks(v), chunks(beta)
    gc = jnp.cumsum(chunks(g), axis=-1)
    causal = jnp.tril(jnp.ones((CHUNK, CHUNK), dtype=bool))
    strict = jnp.tril(jnp.ones((CHUNK, CHUNK), dtype=bool), -1)
    decay = jnp.exp(jnp.where(causal, gc[..., :, None] - gc[..., None, :], -jnp.inf))
    kb = kc * bc[..., None]
    lower = jnp.where(strict, jnp.einsum('nchid,nchjd->nchij', kb, kc) * decay, 0.0)
    eye = jnp.eye(CHUNK, dtype=jnp.float32)
    tmat = lax.linalg.triangular_solve(lower + eye, jnp.broadcast_to(eye, lower.shape),
                                       left_side=True, lower=True, unit_diagonal=True)
    u = jnp.einsum('nchij,nchje->nchie', tmat, vc * bc[..., None])
    w = jnp.einsum('nchij,nchjd->nchid', tmat, kb * jnp.exp(gc)[..., None])

    def step(S, xs):
        q_i, k_i, u_i, w_i, g_i, d_i = xs
        v_new = u_i - jnp.einsum('nhid,nhde->nhie', w_i, S)
        attn = jnp.einsum('nhid,nhjd->nhij', q_i, k_i) * d_i
        o = (jnp.einsum('nhid,nhde->nhie', q_i * jnp.exp(g_i)[..., None], S)
             + jnp.einsum('nhij,nhje->nhie', attn, v_new))
        g_last = g_i[..., -1:]
        S = (S * jnp.exp(g_last)[..., None]
             + jnp.einsum('nhid,nhie->nhde', k_i * jnp.exp(g_last - g_i)[..., None], v_new))
        return S, o

    xs = tuple(jnp.moveaxis(a, 1, 0) for a in (qc, kc, u, w, gc, decay))
    S, o = lax.scan(step, S0, xs)
    o = jnp.swapaxes(jnp.moveaxis(o, 0, 1), 2, 3).reshape(n, L, H, -1)
    return o, S


def gated_delta_recurrent(q, k, v, g, beta, S0):
    scale = q.shape[-1] ** -0.5

    def step(S, xs):
        q_t, k_t, v_t, g_t, b_t = xs
        S = S * jnp.exp(g_t)[..., None, None]
        delta = (v_t - jnp.einsum('nhd,nhde->nhe', k_t, S)) * b_t[..., None]
        S = S + jnp.einsum('nhd,nhe->nhde', k_t, delta)
        return S, jnp.einsum('nhd,nhde->nhe', q_t * scale, S)

    S, o = lax.scan(step, S0, tuple(jnp.moveaxis(a, 1, 0) for a in (q, k, v, g, beta)))
    return jnp.moveaxis(o, 0, 1), S


def delta_mixer(cqkv, z, cb, ca, conv_buf, S0, conv_w, a_log, dt_bias, norm_w, chunked):
    f32 = jnp.float32
    n, L = cqkv.shape[:2]
    xp = jnp.concatenate([conv_buf.astype(cqkv.dtype), cqkv], axis=1)
    conv = xp[:, 0:L] * conv_w[0]
    for i in range(1, CONV_K):
        conv = conv + xp[:, i:i + L] * conv_w[i]
    new_buf = xp[:, L:]
    q, k, v = jnp.split(jax.nn.silu(conv).astype(f32), 3, axis=-1)
    q = l2norm(q.reshape(n, L, C_HEADS, HEAD_DIM))
    k = l2norm(k.reshape(n, L, C_HEADS, HEAD_DIM))
    v = v.reshape(n, L, C_HEADS, HEAD_DIM)
    beta = jax.nn.sigmoid(cb.astype(f32))
    g = -jnp.exp(a_log.astype(f32)) * jax.nn.softplus(ca.astype(f32) + dt_bias.astype(f32))
    S0 = S0.astype(f32)
    if chunked:
        o, S = gated_delta_chunked(q, k, v, g, beta, S0)
    else:
        o, S = gated_delta_recurrent(q, k, v, g, beta, S0)
    o = (o * lax.rsqrt(jnp.mean(o * o, axis=-1, keepdims=True) + EPS) * norm_w.astype(f32)
         * jax.nn.silu(z.astype(f32)))
    return o.reshape(n, L, C_WIDTH), new_buf, S


def project(h, w_in):
    n, L = h.shape[:2]
    p = h @ w_in
    qa, ka, va, qb, kb, vb, cqkv, z, cb, ca, gates = jnp.split(
        p, np.cumsum(PROJ_SIZES)[:-1].tolist(), axis=-1)
    qa = qa.reshape(n, L, A_HEADS, HEAD_DIM)
    ka = ka.reshape(n, L, A_KV, HEAD_DIM)
    va = va.reshape(n, L, A_KV, HEAD_DIM)
    qb = qb.reshape(n, L, N_BGROUPS, B_HPG, HEAD_DIM)
    kb = kb.reshape(n, L, N_BGROUPS, B_KV, HEAD_DIM)
    vb = vb.reshape(n, L, N_BGROUPS, B_KV, HEAD_DIM)
    z = z.reshape(n, L, C_HEADS, HEAD_DIM)
    gates = gates.reshape(n, L, 3, D_MODEL)
    return qa, ka, va, qb, kb, vb, cqkv, z, cb, ca, gates


def merge_branches(oa, ob, oc, gates, w_a, w_b, w_c, w_o):
    n, L = gates.shape[:2]
    dt = w_o.dtype
    ya = oa.reshape(n, L, -1).astype(dt) @ w_a
    yb = ob.reshape(n, L, -1).astype(dt) @ w_b
    yc = oc.reshape(n, L, -1).astype(dt) @ w_c
    gt = jax.nn.sigmoid(gates.astype(jnp.float32)).astype(dt)
    return (gt[:, :, 0] * ya + gt[:, :, 1] * yb + gt[:, :, 2] * yc) @ w_o


def mix_prompt(h, lp):
    w_in, sink, conv_w, a_log, dt_bias, norm_w, w_a, w_b, w_c, w_o = lp
    n = h.shape[0]
    qa, ka, va, qb, kb, vb, cqkv, z, cb, ca, gates = project(h, w_in)
    oa, _ = banded_attention(qa, ka, va, alibi_slopes(A_HEADS), 1, A_WINDOW, sink)
    slopes_b = alibi_slopes(N_BGROUPS * B_HPG).reshape(N_BGROUPS, B_HPG)
    outs, lses, b_state = [], [], []
    for gi, (win, dil) in enumerate(B_GROUPS):
        o, lse = banded_attention(to_stride(qb[:, :, gi], dil), to_stride(kb[:, :, gi], dil),
                                  to_stride(vb[:, :, gi], dil), slopes_b[gi], dil, win // dil)
        outs.append(from_stride(o, dil, n))
        lses.append(from_stride(lse, dil, n))
        b_state += [last_rows(kb[:, :, gi], win), last_rows(vb[:, :, gi], win)]
    ob = combine_groups(outs, lses)
    conv0 = jnp.zeros((n, CONV_K - 1, 3 * C_WIDTH), cqkv.dtype)
    S0 = jnp.zeros((n, C_HEADS, HEAD_DIM, HEAD_DIM), jnp.float32)
    oc, conv_buf, S = delta_mixer(cqkv, z, cb, ca, conv0, S0, conv_w, a_log, dt_bias, norm_w, True)
    y = merge_branches(oa, ob, oc, gates, w_a, w_b, w_c, w_o)
    return y, (last_rows(ka, A_WINDOW), last_rows(va, A_WINDOW), *b_state, conv_buf, S)


def mix_sample(h, lp, st):
    w_in, sink, conv_w, a_log, dt_bias, norm_w, w_a, w_b, w_c, w_o = lp
    c_ak, c_av, b1k, b1v, b2k, b2v, b3k, b3v, conv_buf, S0 = st
    qa, ka, va, qb, kb, vb, cqkv, z, cb, ca, gates = project(h, w_in)
    oa, _, n_ak, n_av = window_decode(qa, c_ak, c_av, ka, va, alibi_slopes(A_HEADS), 1, sink)
    slopes_b = alibi_slopes(N_BGROUPS * B_HPG).reshape(N_BGROUPS, B_HPG)
    b_bufs = ((b1k, b1v), (b2k, b2v), (b3k, b3v))
    outs, lses, b_state = [], [], []
    for gi, (_, dil) in enumerate(B_GROUPS):
        bk, bv = b_bufs[gi]
        o, lse, nbk, nbv = window_decode(qb[:, :, gi], bk, bv, kb[:, :, gi], vb[:, :, gi], slopes_b[gi], dil)
        outs.append(o)
        lses.append(lse)
        b_state += [nbk, nbv]
    ob = combine_groups(outs, lses)
    oc, new_conv, S = delta_mixer(cqkv, z, cb, ca, conv_buf, S0, conv_w, a_log, dt_bias, norm_w, False)
    y = merge_branches(oa, ob, oc, gates, w_a, w_b, w_c, w_o)
    return y, (n_ak, n_av, *b_state, new_conv, S)


def swiglu(h, wg, wu, wd):
    return (jax.nn.silu(h @ wg) * (h @ wu)) @ wd


def moe_ffn(h, router, wg, wu, wd):
    shp = h.shape
    hf = h.reshape(-1, shp[-1])
    logits = (hf @ router).astype(jnp.float32)
    top_v, top_i = lax.top_k(logits, TOP_K)
    gate = jnp.sum(jax.nn.one_hot(top_i, N_EXPERTS, dtype=jnp.float32)
                   * jax.nn.softmax(top_v, axis=-1)[..., None], axis=1).astype(h.dtype)
    y = jnp.zeros_like(hf)
    for e in range(N_EXPERTS):
        y = y + gate[:, e:e + 1] * swiglu(hf, wg[e], wu[e], wd[e])
    return y.reshape(shp)


def setup_inputs(seed: int = 0) -> dict:
    key = jax.random.key(seed)
    keys = iter(jax.random.split(key, 64))
    f32 = jnp.float32

    def nrm(shape, scale):
        return jax.random.normal(next(keys), shape, f32) * scale

    def gain(shape):
        return 1.0 + nrm(shape, 0.02)

    inp = {}
    inp['x_prompt'] = nrm((BATCH, SEQ, D_MODEL), 1.0)
    inp['x_sample'] = nrm((DEC_BATCH, DEC_SEQ, D_MODEL), 1.0)
    inp['cache_a_k'] = nrm((DEPTH, DEC_BATCH, A_WINDOW, A_KV, HEAD_DIM), 1.0)
    inp['cache_a_v'] = nrm((DEPTH, DEC_BATCH, A_WINDOW, A_KV, HEAD_DIM), 1.0)
    for gi, (win, _) in enumerate(B_GROUPS):
        inp['cache_b%d_k' % (gi + 1)] = nrm((DEPTH, DEC_BATCH, win, B_KV, HEAD_DIM), 1.0)
        inp['cache_b%d_v' % (gi + 1)] = nrm((DEPTH, DEC_BATCH, win, B_KV, HEAD_DIM), 1.0)
    inp['state_c_conv'] = nrm((DEPTH, DEC_BATCH, CONV_K - 1, 3 * C_WIDTH), 1.0)
    inp['state_c_rec'] = nrm((DEPTH, DEC_BATCH, C_HEADS, HEAD_DIM, HEAD_DIM), 0.1)
    inp['norm_mix'] = gain((DEPTH, D_MODEL))
    inp['norm_ffn'] = gain((DEPTH, D_MODEL))
    inp['norm_final'] = gain((D_MODEL,))
    inp['w_in'] = nrm((DEPTH, D_MODEL, PROJ_WIDTH), D_MODEL ** -0.5)
    inp['attn_sink'] = nrm((DEPTH, A_HEADS), 0.5)
    inp['conv_w'] = nrm((DEPTH, CONV_K, 3 * C_WIDTH), CONV_K ** -0.5)
    inp['a_log'] = jnp.log(jax.random.uniform(next(keys), (DEPTH, C_HEADS), f32, 1.0, 16.0))
    dt = jnp.exp(jax.random.uniform(next(keys), (DEPTH, C_HEADS), f32, math.log(1e-3), math.log(1e-1)))
    inp['dt_bias'] = dt + jnp.log(-jnp.expm1(-dt))
    inp['norm_delta'] = gain((DEPTH, HEAD_DIM))
    inp['w_out_a'] = nrm((DEPTH, A_HEADS * HEAD_DIM, D_MODEL), (A_HEADS * HEAD_DIM) ** -0.5)
    inp['w_out_b'] = nrm((DEPTH, B_HPG * HEAD_DIM, D_MODEL), (B_HPG * HEAD_DIM) ** -0.5)
    inp['w_out_c'] = nrm((DEPTH, C_WIDTH, D_MODEL), C_WIDTH ** -0.5)
    inp['w_out'] = nrm((DEPTH, D_MODEL, D_MODEL), D_MODEL ** -0.5)
    inp['ffn_w_gate'] = nrm((N_DENSE, D_MODEL, D_FF), D_MODEL ** -0.5)
    inp['ffn_w_up'] = nrm((N_DENSE, D_MODEL, D_FF), D_MODEL ** -0.5)
    inp['ffn_w_down'] = nrm((N_DENSE, D_FF, D_MODEL), D_FF ** -0.5)
    inp['router_w'] = nrm((N_MOE, D_MODEL, N_EXPERTS), D_MODEL ** -0.5)
    inp['moe_w_gate'] = nrm((N_MOE, N_EXPERTS, D_MODEL, D_FF_EXPERT), D_MODEL ** -0.5)
    inp['moe_w_up'] = nrm((N_MOE, N_EXPERTS, D_MODEL, D_FF_EXPERT), D_MODEL ** -0.5)
    inp['moe_w_down'] = nrm((N_MOE, N_EXPERTS, D_FF_EXPERT, D_MODEL), D_FF_EXPERT ** -0.5)
    return inp


def reference(x_prompt, x_sample, cache_a_k, cache_a_v, cache_b1_k, cache_b1_v, cache_b2_k, cache_b2_v,
              cache_b3_k, cache_b3_v, state_c_conv, state_c_rec, norm_mix, norm_ffn, norm_final, w_in,
              attn_sink, conv_w, a_log, dt_bias, norm_delta, w_out_a, w_out_b, w_out_c, w_out,
              ffn_w_gate, ffn_w_up, ffn_w_down, router_w, moe_w_gate, moe_w_up, moe_w_down):
    xp, xs = x_prompt, x_sample
    p_states, s_states = [], []
    for l in range(DEPTH):
        lp = (w_in[l], attn_sink[l], conv_w[l], a_log[l], dt_bias[l], norm_delta[l],
              w_out_a[l], w_out_b[l], w_out_c[l], w_out[l])
        st = (cache_a_k[l], cache_a_v[l], cache_b1_k[l], cache_b1_v[l], cache_b2_k[l], cache_b2_v[l],
              cache_b3_k[l], cache_b3_v[l], state_c_conv[l], state_c_rec[l])
        yp, sp = mix_prompt(rmsnorm(xp, norm_mix[l]), lp)
        ys, ss = mix_sample(rmsnorm(xs, norm_mix[l]), lp, st)
        xp = xp + yp
        xs = xs + ys
        p_states.append(sp)
        s_states.append(ss)
        hp = rmsnorm(xp, norm_ffn[l])
        hs = rmsnorm(xs, norm_ffn[l])
        i = l // 2
        if l % 2 == 0:
            xp = xp + swiglu(hp, ffn_w_gate[i], ffn_w_up[i], ffn_w_down[i])
            xs = xs + swiglu(hs, ffn_w_gate[i], ffn_w_up[i], ffn_w_down[i])
        else:
            xp = xp + moe_ffn(hp, router_w[i], moe_w_gate[i], moe_w_up[i], moe_w_down[i])
            xs = xs + moe_ffn(hs, router_w[i], moe_w_gate[i], moe_w_up[i], moe_w_down[i])
    y_prompt = rmsnorm(xp, norm_final)
    y_sample = rmsnorm(xs, norm_final)
    (p_a_k, p_a_v, p_b1_k, p_b1_v, p_b2_k, p_b2_v, p_b3_k, p_b3_v, p_conv, p_rec) = [
        jnp.stack(z) for z in zip(*p_states)]
    (s_a_k, s_a_v, s_b1_k, s_b1_v, s_b2_k, s_b2_v, s_b3_k, s_b3_v, s_conv, s_rec) = [
        jnp.stack(z) for z in zip(*s_states)]
    return (y_prompt, y_sample,
            p_a_k, p_a_v, p_b1_k, p_b1_v, p_b2_k, p_b2_v, p_b3_k, p_b3_v, p_conv, p_rec,
            s_a_k, s_a_v, s_b1_k, s_b1_v, s_b2_k, s_b2_v, s_b3_k, s_b3_v, s_conv, s_rec)
```

```python
import functools

import numpy as np
import jax
import jax.numpy as jnp
from jax import lax
from jax.experimental import pallas as pl
from jax.experimental.pallas import tpu as pltpu

F32 = jnp.float32
BF16 = jnp.bfloat16
HI = lax.Precision.HIGHEST

D_MODEL = 2048
BATCH = 4
SEQ = 2048
DEPTH = 2
DEC_BATCH = 32
DEC_SEQ = 4
HEAD_DIM = 128
A_HEADS = 8
A_KV = 2
A_WINDOW = 128
B_GROUPS = ((128, 1), (512, 4), (2048, 16))
N_BGROUPS = 3
B_HPG = 8
B_KV = 2
C_HEADS = 8
C_WIDTH = C_HEADS * HEAD_DIM
CONV_K = 4
CHUNK = 64
BAND = 128
D_FF = 5632
N_EXPERTS = 8
D_FF_EXPERT = 7168
PROJ_SIZES = (1024, 256, 256, 3072, 768, 768, 3072, 1024, 8, 8, 6144)
EPS = 1e-6
NEG = -1e30

PW = 16384
DEC_PAD = 8
VMEM_LIMIT_BYTES = 48 * 1024 * 1024


def _cp(*sem):
    return pltpu.CompilerParams(dimension_semantics=sem, vmem_limit_bytes=VMEM_LIMIT_BYTES)


def _alibi(n):
    return np.asarray(2.0 ** (-8.0 * np.arange(1, n + 1) / n), dtype=np.float32)


def _sigmoid(x):
    return 1.0 / (1.0 + jnp.exp(-x))


def _silu(x):
    return x * _sigmoid(x)


def _softplus(x):
    return jnp.maximum(x, 0.0) + jnp.log(1.0 + jnp.exp(-jnp.abs(x)))


def _dot(a, b, prec=None):
    return jnp.dot(a, b, precision=prec, preferred_element_type=F32)


def _dot_nt(a, b, prec=None):
    return lax.dot_general(a, b, (((1,), (1,)), ((), ())), precision=prec, preferred_element_type=F32)


def _dot_tn(a, b, prec=None):
    return lax.dot_general(a, b, (((0,), (0,)), ((), ())), precision=prec, preferred_element_type=F32)


def _rmsnorm_kernel(x_ref, g_ref, o_ref):
    x = x_ref[...]
    ms = jnp.mean(x * x, axis=-1, keepdims=True)
    o_ref[...] = (x * lax.rsqrt(ms + EPS) * g_ref[...]).astype(o_ref.dtype)


def rmsnorm(x, g, out_dtype, tm):
    m, d = x.shape
    return pl.pallas_call(
        _rmsnorm_kernel,
        out_shape=jax.ShapeDtypeStruct((m, d), out_dtype),
        grid=(m // tm,),
        in_specs=[pl.BlockSpec((tm, d), lambda i: (i, 0)), pl.BlockSpec((1, d), lambda i: (0, 0))],
        out_specs=pl.BlockSpec((tm, d), lambda i: (i, 0)),
        compiler_params=_cp("parallel"),
        name="rmsnorm",
    )(x, g.reshape(1, d))


def _prec(x):
    return HI if x.dtype == F32 else None


def _mm_kernel(a_ref, w_ref, o_ref, *, prec):
    o_ref[...] = _dot(a_ref[...], w_ref[...], prec).astype(o_ref.dtype)


def matmul(a, w, tm, tn, out_dtype=F32):
    m, k = a.shape
    n = w.shape[1]
    return pl.pallas_call(
        functools.partial(_mm_kernel, prec=_prec(w)),
        out_shape=jax.ShapeDtypeStruct((m, n), out_dtype),
        grid=(m // tm, n // tn),
        in_specs=[pl.BlockSpec((tm, k), lambda i, j: (i, 0)), pl.BlockSpec((k, tn), lambda i, j: (0, j))],
        out_specs=pl.BlockSpec((tm, tn), lambda i, j: (i, j)),
        compiler_params=_cp("parallel", "arbitrary"),
        name="matmul",
    )(a, w)


def _mm_res_kernel(x_ref, a_ref, w_ref, o_ref, *, prec):
    o_ref[...] = x_ref[...] + _dot(a_ref[...], w_ref[...], prec)


def _mm_res_gate_kernel(x_ref, gate_ref, a_ref, w_ref, o_ref, *, expert, prec):
    gate = gate_ref[:, expert:expert + 1]
    o_ref[...] = x_ref[...] + gate * _dot(a_ref[...], w_ref[...], prec)


def matmul_residual(x, a, w, tm, tn, gate=None, expert=0):
    m, k = a.shape
    n = w.shape[1]
    x_spec = pl.BlockSpec((tm, tn), lambda i, j: (i, j))
    a_spec = pl.BlockSpec((tm, k), lambda i, j: (i, 0))
    w_spec = pl.BlockSpec((k, tn), lambda i, j: (0, j))
    if gate is None:
        body, ins, specs = functools.partial(_mm_res_kernel, prec=_prec(w)), (x, a, w), [x_spec, a_spec, w_spec]
    else:
        body = functools.partial(_mm_res_gate_kernel, expert=expert, prec=_prec(w))
        ins = (x, gate, a, w)
        specs = [x_spec, pl.BlockSpec((tm, 128), lambda i, j: (i, 0)), a_spec, w_spec]
    return pl.pallas_call(
        body,
        out_shape=jax.ShapeDtypeStruct((m, n), F32),
        grid=(m // tm, n // tn),
        in_specs=specs,
        out_specs=pl.BlockSpec((tm, tn), lambda i, j: (i, j)),
        compiler_params=_cp("parallel", "arbitrary"),
        name="matmul_residual",
    )(*ins)


def _swiglu_kernel(h_ref, wg_ref, wu_ref, o_ref, *, prec):
    h = h_ref[...]
    a = _dot(h, wg_ref[...], prec)
    b = _dot(h, wu_ref[...], prec)
    o_ref[...] = (_silu(a) * b).astype(o_ref.dtype)


def swiglu_up(h, wg, wu, tm, tn):
    m, k = h.shape
    n = wg.shape[1]
    return pl.pallas_call(
        functools.partial(_swiglu_kernel, prec=_prec(wg)),
        out_shape=jax.ShapeDtypeStruct((m, n), wg.dtype),
        grid=(m // tm, n // tn),
        in_specs=[pl.BlockSpec((tm, k), lambda i, j: (i, 0)),
                  pl.BlockSpec((k, tn), lambda i, j: (0, j)),
                  pl.BlockSpec((k, tn), lambda i, j: (0, j))],
        out_specs=pl.BlockSpec((tm, tn), lambda i, j: (i, j)),
        compiler_params=_cp("parallel", "arbitrary"),
        name="swiglu_up",
    )(h, wg, wu)


def _band_kernel(*refs, slopes, step, has_sink, want_lse):
    if has_sink:
        sink_ref, refs = refs[0], refs[1:]
    q_ref, kc_ref, kp_ref, vc_ref, vp_ref = refs[:5]
    o_ref = refs[5]
    lse_ref = refs[6] if want_lse else None
    b = pl.program_id(2)
    qi = lax.broadcasted_iota(jnp.int32, (BAND, 2 * BAND), 0)
    ki = lax.broadcasted_iota(jnp.int32, (BAND, 2 * BAND), 1)
    dist = qi + BAND - ki
    mask = (dist >= 0) & (dist <= BAND) & ((ki >= BAND) | (b > 0))
    distf = dist.astype(F32)
    scale = HEAD_DIM ** -0.5
    for g in range(2):
        lo = g * HEAD_DIM
        k = jnp.concatenate([kp_ref[0, :, lo:lo + HEAD_DIM], kc_ref[0, :, lo:lo + HEAD_DIM]], axis=0).astype(BF16)
        v = jnp.concatenate([vp_ref[0, :, lo:lo + HEAD_DIM], vc_ref[0, :, lo:lo + HEAD_DIM]], axis=0).astype(BF16)
        for r in range(4):
            h = g * 4 + r
            q = (q_ref[0, :, h * HEAD_DIM:(h + 1) * HEAD_DIM] * scale).astype(BF16)
            s = _dot_nt(q, k)
            s = jnp.where(mask, s - (float(slopes[h]) * step) * distf, NEG)
            m = jnp.max(s, axis=-1, keepdims=True)
            if has_sink:
                sink = sink_ref[h]
                m = jnp.maximum(m, sink)
            e = jnp.exp(s - m)
            den = jnp.sum(e, axis=-1, keepdims=True)
            if has_sink:
                den = den + jnp.exp(sink - m)
            o = _dot(e.astype(BF16), v) / den
            o_ref[0, :, h * HEAD_DIM:(h + 1) * HEAD_DIM] = o.astype(o_ref.dtype)
            if want_lse:
                lse_ref[0, :, h * HEAD_DIM:(h + 1) * HEAD_DIM] = jnp.broadcast_to(m + jnp.log(den), (BAND, HEAD_DIM))


def band_attention(p, n_seq, seq, dil, qblk, kblk, vblk, slopes, sink, want_lse, out_dtype):
    units = seq // dil
    nb = units // BAND
    pv = p.reshape(n_seq, units, dil * PW)
    qw, kw = PW // 1024, PW // 256
    has_sink = sink is not None
    in_specs = [
        pl.BlockSpec((1, BAND, 1024), lambda n, r, b: (n, b, r * qw + qblk)),
        pl.BlockSpec((1, BAND, 256), lambda n, r, b: (n, b, r * kw + kblk)),
        pl.BlockSpec((1, BAND, 256), lambda n, r, b: (n, jnp.maximum(b - 1, 0), r * kw + kblk)),
        pl.BlockSpec((1, BAND, 256), lambda n, r, b: (n, b, r * kw + vblk)),
        pl.BlockSpec((1, BAND, 256), lambda n, r, b: (n, jnp.maximum(b - 1, 0), r * kw + vblk)),
    ]
    ins = [pv, pv, pv, pv, pv]
    if has_sink:
        in_specs = [pl.BlockSpec(memory_space=pltpu.SMEM)] + in_specs
        ins = [sink] + ins
    o_shape = jax.ShapeDtypeStruct((n_seq, units, dil * 1024), out_dtype)
    o_spec = pl.BlockSpec((1, BAND, 1024), lambda n, r, b: (n, b, r))
    out_shape, out_specs = [o_shape], [o_spec]
    if want_lse:
        out_shape.append(jax.ShapeDtypeStruct((n_seq, units, dil * 1024), F32))
        out_specs.append(o_spec)
    outs = pl.pallas_call(
        functools.partial(_band_kernel, slopes=tuple(float(s) for s in slopes), step=float(dil),
                          has_sink=has_sink, want_lse=want_lse),
        out_shape=out_shape,
        grid=(n_seq, dil, nb),
        in_specs=in_specs,
        out_specs=out_specs,
        compiler_params=_cp("parallel", "parallel", "arbitrary"),
        name="band_attention",
    )(*ins)
    return [o.reshape(n_seq * seq, 1024) for o in outs]


def _decode_kernel(*refs, slopes, dil, win, has_sink):
    if has_sink:
        sink_ref, refs = refs[0], refs[1:]
    q_ref, bk_ref, bv_ref, nk_ref, nv_ref, o_ref, lse_ref = refs
    rows = 4 * DEC_SEQ
    ri = lax.broadcasted_iota(jnp.int32, (rows, win), 0)
    ci = lax.broadcasted_iota(jnp.int32, (rows, win), 1)
    t1 = win + (ri & 3) - ci
    valid1 = (ci >= (ri & 3)) & ((t1 & (dil - 1)) == 0)
    ri2 = lax.broadcasted_iota(jnp.int32, (rows, DEC_PAD), 0)
    ci2 = lax.broadcasted_iota(jnp.int32, (rows, DEC_PAD), 1)
    t2 = (ri2 & 3) - ci2
    valid2 = (t2 >= 0) & ((t2 & (dil - 1)) == 0) & (ci2 < DEC_SEQ)
    hr = lax.broadcasted_iota(jnp.int32, (rows, 1), 0) >> 2
    scale = HEAD_DIM ** -0.5
    for g in range(2):
        lo = g * HEAD_DIM
        slope = jnp.zeros((rows, 1), F32)
        for r in range(4):
            slope = jnp.where(hr == r, float(slopes[g * 4 + r]), slope)
        q = q_ref[0, g] * scale
        s1 = _dot_nt(q, bk_ref[0, :, lo:lo + HEAD_DIM], HI)
        s2 = _dot_nt(q, nk_ref[0, :, lo:lo + HEAD_DIM], HI)
        s1 = jnp.where(valid1, s1 - slope * t1.astype(F32), NEG)
        s2 = jnp.where(valid2, s2 - slope * t2.astype(F32), NEG)
        m = jnp.maximum(jnp.max(s1, axis=-1, keepdims=True), jnp.max(s2, axis=-1, keepdims=True))
        if has_sink:
            sink = jnp.zeros((rows, 1), F32)
            for r in range(4):
                sink = jnp.where(hr == r, sink_ref[g * 4 + r], sink)
            m = jnp.maximum(m, sink)
        e1 = jnp.exp(s1 - m)
        e2 = jnp.exp(s2 - m)
        den = jnp.sum(e1, axis=-1, keepdims=True) + jnp.sum(e2, axis=-1, keepdims=True)
        if has_sink:
            den = den + jnp.exp(sink - m)
        o = _dot(e1, bv_ref[0, :, lo:lo + HEAD_DIM], HI) + _dot(e2, nv_ref[0, :, lo:lo + HEAD_DIM], HI)
        o_ref[0, g] = o / den
        lse_ref[0, g] = jnp.broadcast_to(m + jnp.log(den), (rows, HEAD_DIM))


def window_decode(q, buf_k, buf_v, new_k, new_v, slopes, dil, sink):
    nb, win = buf_k.shape[0], buf_k.shape[1]
    has_sink = sink is not None
    in_specs = [
        pl.BlockSpec((1, 2, 16, HEAD_DIM), lambda b: (b, 0, 0, 0)),
        pl.BlockSpec((1, win, 256), lambda b: (b, 0, 0)),
        pl.BlockSpec((1, win, 256), lambda b: (b, 0, 0)),
        pl.BlockSpec((1, DEC_PAD, 256), lambda b: (b, 0, 0)),
        pl.BlockSpec((1, DEC_PAD, 256), lambda b: (b, 0, 0)),
    ]
    ins = [q, buf_k, buf_v, new_k, new_v]
    if has_sink:
        in_specs = [pl.BlockSpec(memory_space=pltpu.SMEM)] + in_specs
        ins = [sink] + ins
    o_shape = jax.ShapeDtypeStruct((nb, 2, 16, HEAD_DIM), F32)
    o_spec = pl.BlockSpec((1, 2, 16, HEAD_DIM), lambda b: (b, 0, 0, 0))
    return pl.pallas_call(
        functools.partial(_decode_kernel, slopes=tuple(float(s) for s in slopes), dil=dil, win=win,
                          has_sink=has_sink),
        out_shape=[o_shape, o_shape],
        grid=(nb,),
        in_specs=in_specs,
        out_specs=[o_spec, o_spec],
        compiler_params=_cp("parallel"),
        name="window_decode",
    )(*ins)


def _combine_kernel(o1, o2, o3, l1, l2, l3, out_ref):
    a, b, c = l1[...], l2[...], l3[...]
    m = jnp.maximum(jnp.maximum(a, b), c)
    wa, wb, wc = jnp.exp(a - m), jnp.exp(b - m), jnp.exp(c - m)
    tot = wa + wb + wc
    out_ref[...] = ((wa * o1[...] + wb * o2[...] + wc * o3[...]) / tot).astype(out_ref.dtype)


def combine_groups(outs, lses, tm, out_dtype):
    m, d = outs[0].shape
    spec = pl.BlockSpec((tm, d), lambda i: (i, 0))
    return pl.pallas_call(
        _combine_kernel,
        out_shape=jax.ShapeDtypeStruct((m, d), out_dtype),
        grid=(m // tm,),
        in_specs=[spec] * 6,
        out_specs=spec,
        compiler_params=_cp("parallel"),
        name="combine_groups",
    )(*outs, *lses)


def _delta_prep_kernel(x_ref, xp_ref, pc_ref, cw_ref, alog_ref, dtb_ref, eb_ref, eg_ref,
                       q_ref, k_ref, v_ref, bb_ref, gb_ref, *, tm, tiles_per_seq, n_valid):
    i = pl.program_id(0)
    x = x_ref[...]
    prev = xp_ref[...]
    if tiles_per_seq is not None:
        prev = jnp.where(i % tiles_per_seq == 0, 0.0, prev)
    row8 = lax.broadcasted_iota(jnp.int32, (8, 1), 0)
    conv = x * cw_ref[CONV_K - 1:CONV_K, :]
    for back in range(1, CONV_K):
        xr = pltpu.roll(x, back, 0)
        head = jnp.where(row8 < back, pltpu.roll(prev, back, 0), xr[:8])
        sh = head if tm == 8 else jnp.concatenate([head, xr[8:]], axis=0)
        conv = conv + sh * cw_ref[CONV_K - 1 - back:CONV_K - back, :]
    act = _silu(conv)
    pc = pc_ref[...]
    beta = _sigmoid(pc)
    gl = -jnp.exp(alog_ref[...]) * _softplus(pc + dtb_ref[...])
    bb = _dot(beta, eb_ref[...], HI)
    gb = _dot(gl, eg_ref[...], HI)
    if n_valid < tm:
        live = lax.broadcasted_iota(jnp.int32, (tm, 1), 0) < n_valid
        act = jnp.where(live, act, 0.0)
        bb = jnp.where(live, bb, 0.0)
        gb = jnp.where(live, gb, 0.0)
    bb_ref[...] = bb
    gb_ref[...] = gb
    for h in range(C_HEADS):
        lo = h * HEAD_DIM
        qh = act[:, lo:lo + HEAD_DIM]
        kh = act[:, C_WIDTH + lo:C_WIDTH + lo + HEAD_DIM]
        qn = qh * lax.rsqrt(jnp.sum(qh * qh, axis=-1, keepdims=True) + EPS)
        q_ref[:, lo:lo + HEAD_DIM] = qn * (HEAD_DIM ** -0.5)
        k_ref[:, lo:lo + HEAD_DIM] = kh * lax.rsqrt(jnp.sum(kh * kh, axis=-1, keepdims=True) + EPS)
    v_ref[...] = act[:, 2 * C_WIDTH:]


def delta_prep(x_arr, x_colblk, xp_arr, xp_colblk, xp_rowmap, pc, conv_w, a_log, dt_bias, tm, tiles_per_seq, n_valid):
    m = pc.shape[0]
    w3 = 3 * C_WIDTH
    alog = jnp.zeros((1, 128), F32).at[0, 8:16].set(a_log)
    dtb = jnp.zeros((1, 128), F32).at[0, 8:16].set(dt_bias)
    lane_head = np.arange(C_WIDTH) // HEAD_DIM
    eb = jnp.asarray((np.arange(128)[:, None] == lane_head[None, :]).astype(np.float32))
    eg = jnp.asarray((np.arange(128)[:, None] == lane_head[None, :] + 8).astype(np.float32))
    full = lambda shape: pl.BlockSpec(shape, lambda i: (0, 0))
    o_shape = jax.ShapeDtypeStruct((m, C_WIDTH), F32)
    o_spec = pl.BlockSpec((tm, C_WIDTH), lambda i: (i, 0))
    return pl.pallas_call(
        functools.partial(_delta_prep_kernel, tm=tm, tiles_per_seq=tiles_per_seq, n_valid=n_valid),
        out_shape=[o_shape] * 5,
        grid=(m // tm,),
        in_specs=[pl.BlockSpec((tm, w3), lambda i: (i, x_colblk)),
                  pl.BlockSpec((8, w3), lambda i: (xp_rowmap(i), xp_colblk)),
                  pl.BlockSpec((tm, 128), lambda i: (i, 0)),
                  full((CONV_K, w3)), full((1, 128)), full((1, 128)), full((128, C_WIDTH)), full((128, C_WIDTH))],
        out_specs=[o_spec] * 5,
        compiler_params=_cp("parallel"),
        name="delta_prep",
    )(x_arr, xp_arr, pc, conv_w, alog, dtb, eb, eg)


def _unit_lower_inverse(low, c):
    ii = lax.broadcasted_iota(jnp.int32, (c, c), 0)
    jj = lax.broadcasted_iota(jnp.int32, (c, c), 1)
    eye = (ii == jj).astype(F32)
    ld = jnp.where((ii >> 3) == (jj >> 3), low, 0.0)
    l2 = _dot(ld, ld, HI)
    l4 = _dot(l2, l2, HI)
    inv = _dot(_dot(eye - ld, eye + l2, HI), eye + l4, HI)
    s = 3
    while (1 << s) < c:
        off = ((ii >> (s + 1)) == (jj >> (s + 1))) & ((ii >> s) != (jj >> s))
        lo = jnp.where(off, low, 0.0)
        inv = inv - _dot(_dot(inv, lo, HI), inv, HI)
        s += 1
    return inv


def _delta_chunk_kernel(q_ref, k_ref, v_ref, bb_ref, gb_ref, z_ref, s0_ref, nw_ref, o_ref, sout_ref, s_ref, *, c):
    ci = pl.program_id(1)

    @pl.when(ci == 0)
    def _():
        s_ref[...] = s0_ref[0]

    ii = lax.broadcasted_iota(jnp.int32, (c, c), 0)
    jj = lax.broadcasted_iota(jnp.int32, (c, c), 1)
    tri = (ii >= jj).astype(F32)
    sel0 = (lax.broadcasted_iota(jnp.int32, (c, HEAD_DIM), 1) == 0).astype(F32)
    nw = nw_ref[...]
    for h in range(C_HEADS):
        lo = h * HEAD_DIM
        q = q_ref[:, lo:lo + HEAD_DIM]
        k = k_ref[:, lo:lo + HEAD_DIM]
        v = v_ref[:, lo:lo + HEAD_DIM]
        bb = bb_ref[:, lo:lo + HEAD_DIM]
        gcb = _dot(tri, gb_ref[:, lo:lo + HEAD_DIM], HI)
        gcr = _dot_nt(sel0, gcb, HI)
        dec = jnp.where(ii >= jj, jnp.exp(jnp.minimum(gcb[:, :c] - gcr, 0.0)), 0.0)
        kb = k * bb
        low = jnp.where(ii > jj, _dot_nt(kb, k, HI) * dec, 0.0)
        tmat = _unit_lower_inverse(low, c)
        u = _dot(tmat, v * bb, HI)
        w = _dot(tmat, kb * jnp.exp(gcb), HI)
        s = s_ref[h]
        v_new = u - _dot(w, s, HI)
        attn = _dot_nt(q, k, HI) * dec
        o = _dot(q * jnp.exp(gcb), s, HI) + _dot(attn, v_new, HI)
        g_last = gcb[c - 1:c, :]
        s_ref[h] = s * jnp.exp(g_last) + _dot_tn(k * jnp.exp(g_last - gcb), v_new, HI)
        on = o * lax.rsqrt(jnp.mean(o * o, axis=-1, keepdims=True) + EPS) * nw * _silu(z_ref[:, lo:lo + HEAD_DIM])
        o_ref[:, lo:lo + HEAD_DIM] = on.astype(o_ref.dtype)

    @pl.when(ci == pl.num_programs(1) - 1)
    def _():
        sout_ref[0] = s_ref[...]


def delta_chunks(q, k, v, bb, gb, z_arr, z_colblk, s0, norm_w, n_seq, c, out_dtype):
    m = q.shape[0]
    nc = m // n_seq // c
    row = pl.BlockSpec((c, C_WIDTH), lambda n, t: (n * nc + t, 0))
    s_spec = pl.BlockSpec((1, C_HEADS, HEAD_DIM, HEAD_DIM), lambda n, t: (n, 0, 0, 0))
    return pl.pallas_call(
        functools.partial(_delta_chunk_kernel, c=c),
        out_shape=[jax.ShapeDtypeStruct((m, C_WIDTH), out_dtype),
                   jax.ShapeDtypeStruct((n_seq, C_HEADS, HEAD_DIM, HEAD_DIM), F32)],
        grid=(n_seq, nc),
        in_specs=[row, row, row, row, row,
                  pl.BlockSpec((c, C_WIDTH), lambda n, t: (n * nc + t, z_colblk)),
                  s_spec, pl.BlockSpec((1, HEAD_DIM), lambda n, t: (0, 0))],
        out_specs=[row, s_spec],
        scratch_shapes=[pltpu.VMEM((C_HEADS, HEAD_DIM, HEAD_DIM), F32)],
        compiler_params=_cp("parallel", "arbitrary"),
        name="delta_chunks",
    )(q, k, v, bb, gb, z_arr, s0, norm_w.reshape(1, HEAD_DIM))


def _merge_kernel(oa_ref, ob_ref, oc_ref, ga_ref, gb_ref, gc_ref, wa_ref, wb_ref, wc_ref, o_ref, *, prec):
    ya = _dot(oa_ref[...], wa_ref[...], prec)
    yb = _dot(ob_ref[...], wb_ref[...], prec)
    yc = _dot(oc_ref[...], wc_ref[...], prec)
    t = _sigmoid(ga_ref[...]) * ya + _sigmoid(gb_ref[...]) * yb + _sigmoid(gc_ref[...]) * yc
    o_ref[...] = t.astype(o_ref.dtype)


def merge_branches(oa, ob, oc, p, w_a, w_b, w_c, tm, tn):
    m = oa.shape[0]
    gate0 = 10240 // tn
    per = D_MODEL // tn
    o_spec = pl.BlockSpec((tm, 1024), lambda i, j: (i, 0))
    w_spec = pl.BlockSpec((1024, tn), lambda i, j: (0, j))
    gate_spec = lambda b: pl.BlockSpec((tm, tn), lambda i, j: (i, gate0 + b * per + j))
    return pl.pallas_call(
        functools.partial(_merge_kernel, prec=_prec(w_a)),
        out_shape=jax.ShapeDtypeStruct((m, D_MODEL), w_a.dtype),
        grid=(m // tm, D_MODEL // tn),
        in_specs=[o_spec, o_spec, o_spec, gate_spec(0), gate_spec(1), gate_spec(2), w_spec, w_spec, w_spec],
        out_specs=pl.BlockSpec((tm, tn), lambda i, j: (i, j)),
        compiler_params=_cp("parallel", "arbitrary"),
        name="merge_branches",
    )(oa, ob, oc, p, p, p, w_a, w_b, w_c)


def _router_kernel(x_ref, g_ref, r_ref, h_ref, gate_ref):
    x = x_ref[...]
    hn = x * lax.rsqrt(jnp.mean(x * x, axis=-1, keepdims=True) + EPS) * g_ref[...]
    h_ref[...] = hn.astype(h_ref.dtype)
    logits = _dot(hn, r_ref[...], HI)
    lane = lax.broadcasted_iota(jnp.int32, logits.shape, 1).astype(F32)
    logits = jnp.where(lane < N_EXPERTS, logits, NEG)
    v1 = jnp.max(logits, axis=-1, keepdims=True)
    i1 = jnp.min(jnp.where(logits == v1, lane, 128.0), axis=-1, keepdims=True)
    rest = jnp.where(lane == i1, NEG, logits)
    v2 = jnp.max(rest, axis=-1, keepdims=True)
    i2 = jnp.min(jnp.where(rest == v2, lane, 128.0), axis=-1, keepdims=True)
    e = jnp.exp(v2 - v1)
    tot = 1.0 + e
    gate_ref[...] = jnp.where(lane == i1, 1.0 / tot, 0.0) + jnp.where(lane == i2, e / tot, 0.0)


def router(x, g, router_w, tm):
    m, d = x.shape
    rw = jnp.zeros((d, 128), F32).at[:, :N_EXPERTS].set(router_w)
    return pl.pallas_call(
        _router_kernel,
        out_shape=[jax.ShapeDtypeStruct((m, d), BF16), jax.ShapeDtypeStruct((m, 128), F32)],
        grid=(m // tm,),
        in_specs=[pl.BlockSpec((tm, d), lambda i: (i, 0)), pl.BlockSpec((1, d), lambda i: (0, 0)),
                  pl.BlockSpec((d, 128), lambda i: (0, 0))],
        out_specs=[pl.BlockSpec((tm, d), lambda i: (i, 0)), pl.BlockSpec((tm, 128), lambda i: (i, 0))],
        compiler_params=_cp("parallel"),
        name="router",
    )(x, g.reshape(1, d), rw)


def _split_w_in(w):
    qa, ka, va, qb, kb, vb, cqkv, z, cb, ca, gates = jnp.split(w, np.cumsum(PROJ_SIZES)[:-1].tolist(), axis=-1)
    main = jnp.concatenate([qa, qb, ka, va, kb, vb, cqkv, z, gates], axis=-1)
    small = jnp.concatenate([cb, ca, jnp.zeros((w.shape[0], 112), w.dtype)], axis=-1)
    return main, small


def _mix_prompt(h, lp):
    w_main, w_small, sink, conv_w, a_log, dt_bias, norm_w, w_a, w_b, w_c = lp
    m = h.shape[0]
    p = matmul(h, w_main, 512, 512)
    pc = matmul(h, w_small, 512, 128)
    oa, = band_attention(p, BATCH, SEQ, 1, 0, 16, 17, _alibi(A_HEADS), sink, False, BF16)
    slopes_b = _alibi(N_BGROUPS * B_HPG).reshape(N_BGROUPS, B_HPG)
    outs, lses = [], []
    for gi, (_, dil) in enumerate(B_GROUPS):
        o, lse = band_attention(p, BATCH, SEQ, dil, 1 + gi, 18 + gi, 21 + gi, slopes_b[gi], None, True, F32)
        outs.append(o)
        lses.append(lse)
    ob = combine_groups(outs, lses, 512, BF16)
    tm = 256
    q, k, v, bb, gb = delta_prep(p, 2, p, 2, lambda i: jnp.maximum(i * (tm // 8) - 1, 0), pc, conv_w, a_log,
                                 dt_bias, tm, SEQ // tm, tm)
    s0 = jnp.zeros((BATCH, C_HEADS, HEAD_DIM, HEAD_DIM), F32)
    oc, s_fin = delta_chunks(q, k, v, bb, gb, p, 9, s0, norm_w, BATCH, CHUNK, BF16)
    t = merge_branches(oa, ob, oc, p, w_a, w_b, w_c, 512, 512)
    p4 = p.reshape(BATCH, SEQ, PW)
    kv = lambda blk, win: p4[:, SEQ - win:, blk * 256:(blk + 1) * 256].reshape(BATCH, win, 2, HEAD_DIM)
    state = [kv(16, A_WINDOW), kv(17, A_WINDOW)]
    for gi, (win, _) in enumerate(B_GROUPS):
        state += [kv(18 + gi, win), kv(21 + gi, win)]
    state += [p4[:, SEQ - (CONV_K - 1):, 6144:9216], s_fin]
    return t, state


def _mix_sample(h, lp, st):
    w_main, w_small, sink, conv_w, a_log, dt_bias, norm_w, w_a, w_b, w_c = lp
    c_ak, c_av, b1k, b1v, b2k, b2v, b3k, b3v, conv_buf, s0 = st
    nb, nt = DEC_BATCH, DEC_SEQ
    m = nb * nt
    p = matmul(h, w_main, m, 512)
    pc = matmul(h, w_small, m, 128)
    p3 = p.reshape(nb, nt, PW)

    def heads_q(blk):
        qq = p3[:, :, blk * 1024:(blk + 1) * 1024].reshape(nb, nt, 2, 4, HEAD_DIM)
        return jnp.transpose(qq, (0, 2, 3, 1, 4)).reshape(nb, 2, 16, HEAD_DIM)

    def heads_o(o):
        oo = o.reshape(nb, 2, 4, nt, HEAD_DIM)
        return jnp.transpose(oo, (0, 3, 1, 2, 4)).reshape(m, 1024)

    def new_rows(blk):
        x = p3[:, :, blk * 256:(blk + 1) * 256]
        return x, jnp.pad(x, ((0, 0), (0, DEC_PAD - nt), (0, 0)))

    def roll_in(buf, new):
        return jnp.concatenate([buf, new.reshape(nb, nt, 2, HEAD_DIM)], axis=1)[:, nt:]

    flat = lambda buf: buf.reshape(nb, buf.shape[1], 256)
    ka, ka_p = new_rows(16)
    va, va_p = new_rows(17)
    oa, _ = window_decode(heads_q(0), flat(c_ak), flat(c_av), ka_p, va_p, _alibi(A_HEADS), 1, sink)
    oa = heads_o(oa)
    state = [roll_in(c_ak, ka), roll_in(c_av, va)]
    slopes_b = _alibi(N_BGROUPS * B_HPG).reshape(N_BGROUPS, B_HPG)
    outs, lses = [], []
    for gi, ((_, dil), (bk, bv)) in enumerate(zip(B_GROUPS, ((b1k, b1v), (b2k, b2v), (b3k, b3v)))):
        kn, kn_p = new_rows(18 + gi)
        vn, vn_p = new_rows(21 + gi)
        o, lse = window_decode(heads_q(1 + gi), flat(bk), flat(bv), kn_p, vn_p, slopes_b[gi], dil, None)
        outs.append(heads_o(o))
        lses.append(heads_o(lse))
        state += [roll_in(bk, kn), roll_in(bv, vn)]
    ob = combine_groups(outs, lses, m, F32)

    pad_rows = lambda x: jnp.pad(x.reshape(nb, nt, -1), ((0, 0), (0, DEC_PAD - nt), (0, 0))).reshape(nb * DEC_PAD, -1)
    cq = p3[:, :, 6144:9216]
    xs = pad_rows(cq)
    xprev = jnp.pad(conv_buf, ((0, 0), (8 - (CONV_K - 1), 0), (0, 0))).reshape(nb * 8, 3 * C_WIDTH)
    q, k, v, bb, gb = delta_prep(xs, 0, xprev, 0, lambda i: i, pad_rows(pc), conv_w, a_log, dt_bias,
                                 DEC_PAD, None, nt)
    zs = pad_rows(p3[:, :, 9216:10240])
    oc, s_fin = delta_chunks(q, k, v, bb, gb, zs, 0, s0, norm_w, nb, DEC_PAD, F32)
    oc = oc.reshape(nb, DEC_PAD, C_WIDTH)[:, :nt].reshape(m, C_WIDTH)
    t = merge_branches(oa, ob, oc, p, w_a, w_b, w_c, m, 512)
    new_conv = jnp.concatenate([conv_buf, cq], axis=1)[:, nt:]
    state += [new_conv, s_fin]
    return t, state


def kernel(x_prompt, x_sample, cache_a_k, cache_a_v, cache_b1_k, cache_b1_v, cache_b2_k, cache_b2_v, cache_b3_k, cache_b3_v, state_c_conv, state_c_rec, norm_mix, norm_ffn, norm_final, w_in, attn_sink, conv_w, a_log, dt_bias, norm_delta, w_out_a, w_out_b, w_out_c, w_out, ffn_w_gate, ffn_w_up, ffn_w_down, router_w, moe_w_gate, moe_w_up, moe_w_down):
    mp, ms = BATCH * SEQ, DEC_BATCH * DEC_SEQ
    xp = x_prompt.reshape(mp, D_MODEL)
    xs = x_sample.reshape(ms, D_MODEL)
    tmp = 512
    p_states, s_states = [], []
    for l in range(DEPTH):
        w_main, w_small = _split_w_in(w_in[l])
        shared = (attn_sink[l], conv_w[l], a_log[l], dt_bias[l], norm_delta[l])
        lp_s = (w_main, w_small, *shared, w_out_a[l], w_out_b[l], w_out_c[l])
        lp_p = (w_main.astype(BF16), w_small.astype(BF16), *shared,
                w_out_a[l].astype(BF16), w_out_b[l].astype(BF16), w_out_c[l].astype(BF16))
        st = (cache_a_k[l], cache_a_v[l], cache_b1_k[l], cache_b1_v[l], cache_b2_k[l], cache_b2_v[l],
              cache_b3_k[l], cache_b3_v[l], state_c_conv[l], state_c_rec[l])
        tp, sp = _mix_prompt(rmsnorm(xp, norm_mix[l], BF16, tmp), lp_p)
        ts, ss = _mix_sample(rmsnorm(xs, norm_mix[l], F32, ms), lp_s, st)
        xp = matmul_residual(xp, tp, w_out[l].astype(BF16), tmp, 512)
        xs = matmul_residual(xs, ts, w_out[l], ms, 512)
        p_states.append(sp)
        s_states.append(ss)
        i = l // 2
        if l % 2 == 0:
            wg, wu, wd = ffn_w_gate[i], ffn_w_up[i], ffn_w_down[i]
            hp = rmsnorm(xp, norm_ffn[l], BF16, tmp)
            hs = rmsnorm(xs, norm_ffn[l], F32, ms)
            xp = matmul_residual(xp, swiglu_up(hp, wg.astype(BF16), wu.astype(BF16), tmp, 512), wd.astype(BF16),
                                 tmp, 512)
            xs = matmul_residual(xs, swiglu_up(hs, wg, wu, ms, 512), wd, ms, 512)
        else:
            hp, gate_p = router(xp, norm_ffn[l], router_w[i], tmp)
            hs, gate_s = router(xs, norm_ffn[l], router_w[i], ms)
            for e in range(N_EXPERTS):
                wg, wu, wd = (moe_w_gate[i, e].astype(BF16), moe_w_up[i, e].astype(BF16),
                              moe_w_down[i, e].astype(BF16))
                xp = matmul_residual(xp, swiglu_up(hp, wg, wu, tmp, 512), wd, tmp, 512, gate_p, e)
                xs = matmul_residual(xs, swiglu_up(hs, wg, wu, ms, 512), wd, ms, 512, gate_s, e)
    y_prompt = rmsnorm(xp, norm_final, F32, tmp).reshape(BATCH, SEQ, D_MODEL)
    y_sample = rmsnorm(xs, norm_final, F32, ms).reshape(DEC_BATCH, DEC_SEQ, D_MODEL)
    p_out = [jnp.stack(z) for z in zip(*p_states)]
    s_out = [jnp.stack(z) for z in zip(*s_states)]
    return (y_prompt, y_sample, *p_out, *s_out)
```

```python
import functools

import numpy as np
import jax
import jax.numpy as jnp
from jax import lax
from jax.experimental import pallas as pl
from jax.experimental.pallas import tpu as pltpu

F32 = jnp.float32
BF16 = jnp.bfloat16
HI = lax.Precision.HIGHEST

D_MODEL = 2048
BATCH = 4
SEQ = 2048
DEPTH = 2
DEC_BATCH = 32
DEC_SEQ = 4
HEAD_DIM = 128
A_HEADS = 8
A_KV = 2
A_WINDOW = 128
B_GROUPS = ((128, 1), (512, 4), (2048, 16))
N_BGROUPS = 3
B_HPG = 8
B_KV = 2
C_HEADS = 8
C_WIDTH = C_HEADS * HEAD_DIM
CONV_K = 4
CHUNK = 64
BAND = 128
D_FF = 5632
N_EXPERTS = 8
D_FF_EXPERT = 7168
PROJ_SIZES = (1024, 256, 256, 3072, 768, 768, 3072, 1024, 8, 8, 6144)
EPS = 1e-6
NEG = -1e30

PW = 16384
DEC_PAD = 8
VMEM_LIMIT_BYTES = 56 * 1024 * 1024


def _cp(*sem):
    return pltpu.CompilerParams(dimension_semantics=sem, vmem_limit_bytes=VMEM_LIMIT_BYTES)


def _alibi(n):
    return np.asarray(2.0 ** (-8.0 * np.arange(1, n + 1) / n), dtype=np.float32)


def _sigmoid(x):
    return 1.0 / (1.0 + jnp.exp(-x))


def _silu(x):
    return x * _sigmoid(x)


def _softplus(x):
    return jnp.maximum(x, 0.0) + jnp.log(1.0 + jnp.exp(-jnp.abs(x)))


def _dot(a, b, prec=None):
    return jnp.dot(a, b, precision=prec, preferred_element_type=F32)


def _dot_nt(a, b, prec=None):
    return lax.dot_general(a, b, (((1,), (1,)), ((), ())), precision=prec, preferred_element_type=F32)


def _dot_tn(a, b, prec=None):
    return lax.dot_general(a, b, (((0,), (0,)), ((), ())), precision=prec, preferred_element_type=F32)


def _rmsnorm_kernel(x_ref, g_ref, o_ref):
    x = x_ref[...]
    ms = jnp.mean(x * x, axis=-1, keepdims=True)
    o_ref[...] = (x * lax.rsqrt(ms + EPS) * g_ref[...]).astype(o_ref.dtype)


def rmsnorm(x, g, out_dtype, tm):
    m, d = x.shape
    return pl.pallas_call(
        _rmsnorm_kernel,
        out_shape=jax.ShapeDtypeStruct((m, d), out_dtype),
        grid=(m // tm,),
        in_specs=[pl.BlockSpec((tm, d), lambda i: (i, 0)), pl.BlockSpec((1, d), lambda i: (0, 0))],
        out_specs=pl.BlockSpec((tm, d), lambda i: (i, 0)),
        compiler_params=_cp("parallel"),
        name="rmsnorm",
    )(x, g.reshape(1, d))


def _prec(x):
    return HI if x.dtype == F32 else None


def _mm_kernel(a_ref, w_ref, o_ref, *, prec):
    o_ref[...] = _dot(a_ref[...], w_ref[...], prec).astype(o_ref.dtype)


def matmul(a, w, tm, tn, out_dtype=F32):
    m, k = a.shape
    n = w.shape[1]
    return pl.pallas_call(
        functools.partial(_mm_kernel, prec=_prec(w)),
        out_shape=jax.ShapeDtypeStruct((m, n), out_dtype),
        grid=(m // tm, n // tn),
        in_specs=[pl.BlockSpec((tm, k), lambda i, j: (i, 0)), pl.BlockSpec((k, tn), lambda i, j: (0, j))],
        out_specs=pl.BlockSpec((tm, tn), lambda i, j: (i, j)),
        compiler_params=_cp("parallel", "arbitrary"),
        name="matmul",
    )(a, w)


def _mm_res_kernel(x_ref, a_ref, w_ref, o_ref, *, prec):
    o_ref[...] = x_ref[...] + _dot(a_ref[...], w_ref[...], prec)


def matmul_residual(x, a, w, tm, tn):
    m, k = a.shape
    n = w.shape[1]
    return pl.pallas_call(
        functools.partial(_mm_res_kernel, prec=_prec(w)),
        out_shape=jax.ShapeDtypeStruct((m, n), F32),
        grid=(m // tm, n // tn),
        in_specs=[pl.BlockSpec((tm, tn), lambda i, j: (i, j)),
                  pl.BlockSpec((tm, k), lambda i, j: (i, 0)),
                  pl.BlockSpec((k, tn), lambda i, j: (0, j))],
        out_specs=pl.BlockSpec((tm, tn), lambda i, j: (i, j)),
        compiler_params=_cp("parallel", "arbitrary"),
        name="matmul_residual",
    )(x, a, w)


def _swiglu_kernel(h_ref, wg_ref, wu_ref, o_ref, *, prec):
    h = h_ref[...]
    a = _dot(h, wg_ref[...], prec)
    b = _dot(h, wu_ref[...], prec)
    o_ref[...] = (_silu(a) * b).astype(o_ref.dtype)


def swiglu_up(h, wg, wu, tm, tn):
    m, k = h.shape
    n = wg.shape[1]
    return pl.pallas_call(
        functools.partial(_swiglu_kernel, prec=_prec(wg)),
        out_shape=jax.ShapeDtypeStruct((m, n), wg.dtype),
        grid=(m // tm, n // tn),
        in_specs=[pl.BlockSpec((tm, k), lambda i, j: (i, 0)),
                  pl.BlockSpec((k, tn), lambda i, j: (0, j)),
                  pl.BlockSpec((k, tn), lambda i, j: (0, j))],
        out_specs=pl.BlockSpec((tm, tn), lambda i, j: (i, j)),
        compiler_params=_cp("parallel", "arbitrary"),
        name="swiglu_up",
    )(h, wg, wu)


def _band_kernel(slope_ref, sink_ref, q_ref, k_ref, v_ref, o_ref, *lse_refs, dil, has_sink):
    h = pl.program_id(1)
    seq = q_ref.shape[1]
    span = BAND * dil
    qi = lax.broadcasted_iota(jnp.int32, (BAND, 2 * BAND), 0)
    ki = lax.broadcasted_iota(jnp.int32, (BAND, 2 * BAND), 1)
    dist = qi + BAND - ki
    band = (dist >= 0) & (dist <= BAND)
    distf = dist.astype(F32) * float(dil)
    scale = HEAD_DIM ** -0.5

    def block(it, carry):
        r = it % dil
        b = it // dil
        cur = pl.ds(b * span + r, BAND, stride=dil)
        prv = pl.ds(jnp.maximum(b - 1, 0) * span + r, BAND, stride=dil)
        k = jnp.concatenate([k_ref[0, prv, :], k_ref[0, cur, :]], axis=0).astype(BF16)
        v = jnp.concatenate([v_ref[0, prv, :], v_ref[0, cur, :]], axis=0).astype(BF16)
        mask = band & ((ki >= BAND) | (b > 0))
        q = (q_ref[0, cur, :] * scale).astype(BF16)
        s = jnp.where(mask, _dot_nt(q, k) - slope_ref[h] * distf, NEG)
        m = jnp.max(s, axis=-1, keepdims=True)
        if has_sink:
            sink = sink_ref[h]
            m = jnp.maximum(m, sink)
        e = jnp.exp(s - m)
        den = jnp.sum(e, axis=-1, keepdims=True)
        if has_sink:
            den = den + jnp.exp(sink - m)
        o_ref[0, cur, :] = (_dot(e.astype(BF16), v) / den).astype(o_ref.dtype)
        for lse_ref in lse_refs:
            lse_ref[0, cur, :] = jnp.broadcast_to(m + jnp.log(den), (BAND, HEAD_DIM))
        return carry

    lax.fori_loop(0, seq // BAND, block, 0, unroll=8)


def band_attention(p3, dil, qblk, kblk, vblk, slopes, sink, want_lse, out_dtype):
    n_seq, seq, _ = p3.shape
    has_sink = sink is not None
    smem = pl.BlockSpec(memory_space=pltpu.SMEM)
    kv_spec = lambda blk: pl.BlockSpec((1, seq, HEAD_DIM), lambda n, h: (n, 0, 2 * blk + h // 4))
    o_spec = pl.BlockSpec((1, seq, HEAD_DIM), lambda n, h: (n, 0, h))
    out_shape = [jax.ShapeDtypeStruct((n_seq, seq, 1024), out_dtype)]
    if want_lse:
        out_shape.append(jax.ShapeDtypeStruct((n_seq, seq, 1024), F32))
    outs = pl.pallas_call(
        functools.partial(_band_kernel, dil=dil, has_sink=has_sink),
        out_shape=out_shape,
        grid=(n_seq, 8),
        in_specs=[smem, smem, pl.BlockSpec((1, seq, HEAD_DIM), lambda n, h: (n, 0, 8 * qblk + h)),
                  kv_spec(kblk), kv_spec(vblk)],
        out_specs=[o_spec] * len(out_shape),
        compiler_params=_cp("parallel", "arbitrary"),
        name="band_attention",
    )(jnp.asarray(slopes, F32), sink if has_sink else jnp.zeros((A_HEADS,), F32), p3, p3, p3)
    return [o.reshape(n_seq * seq, 1024) for o in outs]


def _decode_kernel(*refs, slopes, dil, win, has_sink):
    if has_sink:
        sink_ref, refs = refs[0], refs[1:]
    q_ref, bk_ref, bv_ref, nk_ref, nv_ref, o_ref, lse_ref = refs
    rows = 4 * DEC_SEQ
    keep = min(dil, DEC_SEQ)
    held = win // dil * keep
    ri = lax.broadcasted_iota(jnp.int32, (rows, held), 0)
    ci = lax.broadcasted_iota(jnp.int32, (rows, held), 1)
    wpos = (ci // keep) * dil + ci % keep
    t1 = win + (ri & 3) - wpos
    valid1 = (wpos >= (ri & 3)) & ((t1 & (dil - 1)) == 0)
    ri2 = lax.broadcasted_iota(jnp.int32, (rows, DEC_PAD), 0)
    ci2 = lax.broadcasted_iota(jnp.int32, (rows, DEC_PAD), 1)
    t2 = (ri2 & 3) - ci2
    valid2 = (t2 >= 0) & ((t2 & (dil - 1)) == 0) & (ci2 < DEC_SEQ)
    hr = lax.broadcasted_iota(jnp.int32, (rows, 1), 0) >> 2
    scale = HEAD_DIM ** -0.5
    for g in range(2):
        lo = g * HEAD_DIM
        slope = jnp.zeros((rows, 1), F32)
        for r in range(4):
            slope = jnp.where(hr == r, float(slopes[g * 4 + r]), slope)
        q = q_ref[0, g] * scale
        s1 = _dot_nt(q, bk_ref[0, :, lo:lo + HEAD_DIM], HI)
        s2 = _dot_nt(q, nk_ref[0, :, lo:lo + HEAD_DIM], HI)
        s1 = jnp.where(valid1, s1 - slope * t1.astype(F32), NEG)
        s2 = jnp.where(valid2, s2 - slope * t2.astype(F32), NEG)
        m = jnp.maximum(jnp.max(s1, axis=-1, keepdims=True), jnp.max(s2, axis=-1, keepdims=True))
        if has_sink:
            sink = jnp.zeros((rows, 1), F32)
            for r in range(4):
                sink = jnp.where(hr == r, sink_ref[g * 4 + r], sink)
            m = jnp.maximum(m, sink)
        e1 = jnp.exp(s1 - m)
        e2 = jnp.exp(s2 - m)
        den = jnp.sum(e1, axis=-1, keepdims=True) + jnp.sum(e2, axis=-1, keepdims=True)
        if has_sink:
            den = den + jnp.exp(sink - m)
        o = _dot(e1, bv_ref[0, :, lo:lo + HEAD_DIM], HI) + _dot(e2, nv_ref[0, :, lo:lo + HEAD_DIM], HI)
        o_ref[0, g] = o / den
        lse_ref[0, g] = jnp.broadcast_to(m + jnp.log(den), (rows, HEAD_DIM))


def window_decode(q, buf_k, buf_v, layer, win, new_k, new_v, slopes, dil, sink):
    nb, held = q.shape[0], buf_k.shape[1]
    has_sink = sink is not None
    in_specs = [
        pl.BlockSpec((1, 2, 16, HEAD_DIM), lambda b: (b, 0, 0, 0)),
        pl.BlockSpec((1, held, 256), lambda b: (layer * nb + b, 0, 0)),
        pl.BlockSpec((1, held, 256), lambda b: (layer * nb + b, 0, 0)),
        pl.BlockSpec((1, DEC_PAD, 256), lambda b: (b, 0, 0)),
        pl.BlockSpec((1, DEC_PAD, 256), lambda b: (b, 0, 0)),
    ]
    ins = [q, buf_k, buf_v, new_k, new_v]
    if has_sink:
        in_specs = [pl.BlockSpec(memory_space=pltpu.SMEM)] + in_specs
        ins = [sink] + ins
    o_shape = jax.ShapeDtypeStruct((nb, 2, 16, HEAD_DIM), F32)
    o_spec = pl.BlockSpec((1, 2, 16, HEAD_DIM), lambda b: (b, 0, 0, 0))
    return pl.pallas_call(
        functools.partial(_decode_kernel, slopes=tuple(float(s) for s in slopes), dil=dil, win=win,
                          has_sink=has_sink),
        out_shape=[o_shape, o_shape],
        grid=(nb,),
        in_specs=in_specs,
        out_specs=[o_spec, o_spec],
        compiler_params=_cp("parallel"),
        name="window_decode",
    )(*ins)


def _combine_kernel(o1, o2, o3, l1, l2, l3, out_ref):
    a, b, c = l1[...], l2[...], l3[...]
    m = jnp.maximum(jnp.maximum(a, b), c)
    wa, wb, wc = jnp.exp(a - m), jnp.exp(b - m), jnp.exp(c - m)
    tot = wa + wb + wc
    out_ref[...] = ((wa * o1[...] + wb * o2[...] + wc * o3[...]) / tot).astype(out_ref.dtype)


def combine_groups(outs, lses, tm, out_dtype):
    m, d = outs[0].shape
    spec = pl.BlockSpec((tm, d), lambda i: (i, 0))
    return pl.pallas_call(
        _combine_kernel,
        out_shape=jax.ShapeDtypeStruct((m, d), out_dtype),
        grid=(m // tm,),
        in_specs=[spec] * 6,
        out_specs=spec,
        compiler_params=_cp("parallel"),
        name="combine_groups",
    )(*outs, *lses)


def _delta_prep_kernel(x_ref, xp_ref, pc_ref, cw_ref, alog_ref, dtb_ref, eb_ref, eg_ref,
                       q_ref, k_ref, v_ref, bb_ref, gb_ref, *, tm, tiles_per_seq, n_valid):
    i = pl.program_id(0)
    x = x_ref[...]
    prev = xp_ref[...]
    if tiles_per_seq is not None:
        prev = jnp.where(i % tiles_per_seq == 0, 0.0, prev)
    row8 = lax.broadcasted_iota(jnp.int32, (8, 1), 0)
    conv = x * cw_ref[CONV_K - 1:CONV_K, :]
    for back in range(1, CONV_K):
        xr = pltpu.roll(x, back, 0)
        head = jnp.where(row8 < back, pltpu.roll(prev, back, 0), xr[:8])
        sh = head if tm == 8 else jnp.concatenate([head, xr[8:]], axis=0)
        conv = conv + sh * cw_ref[CONV_K - 1 - back:CONV_K - back, :]
    act = _silu(conv)
    pc = pc_ref[...]
    beta = _sigmoid(pc)
    gl = -jnp.exp(alog_ref[...]) * _softplus(pc + dtb_ref[...])
    bb = _dot(beta, eb_ref[...], HI)
    gb = _dot(gl, eg_ref[...], HI)
    if n_valid < tm:
        live = lax.broadcasted_iota(jnp.int32, (tm, 1), 0) < n_valid
        act = jnp.where(live, act, 0.0)
        bb = jnp.where(live, bb, 0.0)
        gb = jnp.where(live, gb, 0.0)
    bb_ref[...] = bb
    gb_ref[...] = gb
    for h in range(C_HEADS):
        lo = h * HEAD_DIM
        qh = act[:, lo:lo + HEAD_DIM]
        kh = act[:, C_WIDTH + lo:C_WIDTH + lo + HEAD_DIM]
        qn = qh * lax.rsqrt(jnp.sum(qh * qh, axis=-1, keepdims=True) + EPS)
        q_ref[:, lo:lo + HEAD_DIM] = qn * (HEAD_DIM ** -0.5)
        k_ref[:, lo:lo + HEAD_DIM] = kh * lax.rsqrt(jnp.sum(kh * kh, axis=-1, keepdims=True) + EPS)
    v_ref[...] = act[:, 2 * C_WIDTH:]


def delta_prep(x_arr, x_colblk, xp_arr, xp_colblk, xp_rowmap, pc, conv_w, a_log, dt_bias, tm, tiles_per_seq, n_valid):
    m = pc.shape[0]
    w3 = 3 * C_WIDTH
    alog = jnp.zeros((1, 128), F32).at[0, 8:16].set(a_log)
    dtb = jnp.zeros((1, 128), F32).at[0, 8:16].set(dt_bias)
    lane_head = np.arange(C_WIDTH) // HEAD_DIM
    eb = jnp.asarray((np.arange(128)[:, None] == lane_head[None, :]).astype(np.float32))
    eg = jnp.asarray((np.arange(128)[:, None] == lane_head[None, :] + 8).astype(np.float32))
    full = lambda shape: pl.BlockSpec(shape, lambda i: (0, 0))
    o_shape = jax.ShapeDtypeStruct((m, C_WIDTH), F32)
    o_spec = pl.BlockSpec((tm, C_WIDTH), lambda i: (i, 0))
    return pl.pallas_call(
        functools.partial(_delta_prep_kernel, tm=tm, tiles_per_seq=tiles_per_seq, n_valid=n_valid),
        out_shape=[o_shape] * 5,
        grid=(m // tm,),
        in_specs=[pl.BlockSpec((tm, w3), lambda i: (i, x_colblk)),
                  pl.BlockSpec((8, w3), lambda i: (xp_rowmap(i), xp_colblk)),
                  pl.BlockSpec((tm, 128), lambda i: (i, 0)),
                  full((CONV_K, w3)), full((1, 128)), full((1, 128)), full((128, C_WIDTH)), full((128, C_WIDTH))],
        out_specs=[o_spec] * 5,
        compiler_params=_cp("parallel"),
        name="delta_prep",
    )(x_arr, xp_arr, pc, conv_w, alog, dtb, eb, eg)


def _dot_hi(a, b):
    return _dot(a, b, HI)


def _unit_lower_inverse(low, c, mm=_dot_hi):
    ii = lax.broadcasted_iota(jnp.int32, (c, c), 0)
    jj = lax.broadcasted_iota(jnp.int32, (c, c), 1)
    eye = (ii == jj).astype(F32)
    ld = jnp.where((ii >> 3) == (jj >> 3), low, 0.0)
    l2 = mm(ld, ld)
    l4 = mm(l2, l2)
    inv = mm(mm(eye - ld, eye + l2), eye + l4)
    s = 3
    while (1 << s) < c:
        off = ((ii >> (s + 1)) == (jj >> (s + 1))) & ((ii >> s) != (jj >> s))
        lo = jnp.where(off, low, 0.0)
        inv = inv - mm(mm(inv, lo), inv)
        s += 1
    return inv


def _split_bf16(x, n):
    parts, rest = [], x
    for _ in range(n):
        part = rest.astype(BF16)
        parts.append(part)
        rest = rest - part.astype(F32)
    return parts


def _dot_exact_lhs(a, b, nt=False):
    mm = _dot_nt if nt else _dot
    hi, mid, lo = _split_bf16(b, 3)
    return mm(a, hi) + (mm(a, mid) + mm(a, lo))


def _bdot(a, b):
    return lax.dot_general(a, b, (((2,), (1,)), ((0,), (0,))), preferred_element_type=F32)


def _bdot_nt(a, b):
    return lax.dot_general(a, b, (((2,), (2,)), ((0,), (0,))), preferred_element_type=F32)


def _bdot3(a, b):
    a_hi, a_lo = _split_bf16(a, 2)
    b_hi, b_lo = _split_bf16(b, 2)
    return _bdot(a_hi, b_hi) + (_bdot(a_hi, b_lo) + _bdot(a_lo, b_hi))


def _delta_wy_kernel(q_ref, k_ref, v_ref, bb_ref, gb_ref, u_ref, w_ref, at_ref, qg_ref, kd_ref, egl_ref, *, nch):
    c = CHUNK
    ii = lax.broadcasted_iota(jnp.int32, (c, c), 0)
    jj = lax.broadcasted_iota(jnp.int32, (c, c), 1)
    tri = (ii >= jj).astype(BF16)
    sel0 = (lax.broadcasted_iota(jnp.int32, (c, HEAD_DIM), 1) == 0).astype(BF16)
    heads = [slice(h * HEAD_DIM, (h + 1) * HEAD_DIM) for h in range(C_HEADS)]
    for t in range(nch):
        rows = slice(t * c, (t + 1) * c)
        q, k, v, bb = q_ref[rows, :], k_ref[rows, :], v_ref[rows, :], bb_ref[rows, :]
        gcb = _dot_exact_lhs(tri, gb_ref[rows, :])
        eg = jnp.exp(gcb)
        g_last = gcb[c - 1:c, :]
        kd_ref[rows, :] = (k * jnp.exp(g_last - gcb)).astype(BF16)
        qg_ref[rows, :] = (q * eg).astype(BF16)
        egl_ref[t] = jnp.broadcast_to(jnp.exp(g_last), (8, C_WIDTH))
        kb = k * bb
        vb = v * bb
        kbe = kb * eg
        gcr = jnp.stack([_dot_exact_lhs(sel0, gcb[:, hs], nt=True) for hs in heads])
        gci = jnp.stack([gcb[:, hs][:, :c] for hs in heads])
        dec = jnp.where(ii >= jj, jnp.exp(jnp.minimum(gci - gcr, 0.0)), 0.0)
        lhs = jnp.stack([jnp.concatenate([kb[:, hs], q[:, hs]], axis=0) for hs in heads]).astype(BF16)
        qk = _bdot_nt(lhs, jnp.stack([k[:, hs] for hs in heads]).astype(BF16))
        low = jnp.where(ii > jj, qk[:, :c] * dec, 0.0)
        attn = qk[:, c:] * dec
        tmat = _unit_lower_inverse(low, c, _bdot3)
        rhs = jnp.stack([jnp.concatenate([vb[:, hs], kbe[:, hs]], axis=1) for hs in heads]).astype(BF16)
        uw = _bdot(tmat.astype(BF16), rhs)
        for h, hs in enumerate(heads):
            u_ref[rows, hs] = uw[h, :, :HEAD_DIM]
            w_ref[rows, hs] = uw[h, :, HEAD_DIM:].astype(BF16)
            at_ref[rows, h * c:(h + 1) * c] = attn[h].astype(BF16)


def delta_wy(q, k, v, bb, gb, nch):
    m = q.shape[0]
    rows = nch * CHUNK
    row = pl.BlockSpec((rows, C_WIDTH), lambda i: (i, 0))
    sds = lambda w, dt: jax.ShapeDtypeStruct((m, w), dt)
    return pl.pallas_call(
        functools.partial(_delta_wy_kernel, nch=nch),
        out_shape=[sds(C_WIDTH, F32), sds(C_WIDTH, BF16), sds(C_HEADS * CHUNK, BF16), sds(C_WIDTH, BF16),
                   sds(C_WIDTH, BF16), jax.ShapeDtypeStruct((m // CHUNK, 8, C_WIDTH), F32)],
        grid=(m // rows,),
        in_specs=[row] * 5,
        out_specs=[row, row, pl.BlockSpec((rows, C_HEADS * CHUNK), lambda i: (i, 0)), row, row,
                   pl.BlockSpec((nch, 8, C_WIDTH), lambda i: (i, 0, 0))],
        compiler_params=_cp("parallel"),
        name="delta_wy",
    )(q, k, v, bb, gb)


def _delta_rec_kernel(u_ref, w_ref, at_ref, qg_ref, kd_ref, egl_ref, z_ref, s0_ref, nw_ref, o_ref, sout_ref, s_ref,
                      *, n_seq):
    ci = pl.program_id(0)
    c = CHUNK

    @pl.when(ci == 0)
    def _():
        s_ref[...] = s0_ref[...]

    nw = nw_ref[...]
    for n in range(n_seq):
        idx = [(n * C_HEADS + h, slice(h * HEAD_DIM, (h + 1) * HEAD_DIM)) for h in range(C_HEADS)]
        st = [s_ref[i] for i, _ in idx]
        r = [_dot(jnp.concatenate([w_ref[n, :, hs], qg_ref[n, :, hs]], axis=0), s.astype(BF16))
             for (_, hs), s in zip(idx, st)]
        vn = [(u_ref[n, :, hs] - rr[:c]).astype(BF16) for (_, hs), rr in zip(idx, r)]
        for h, ((i, hs), s, rr, vb) in enumerate(zip(idx, st, r, vn)):
            o = rr[c:] + _dot(at_ref[n, :, h * c:(h + 1) * c], vb)
            s_ref[i] = s * egl_ref[n, 0, 0:1, hs] + _dot_tn(kd_ref[n, :, hs], vb)
            on = o * lax.rsqrt(jnp.mean(o * o, axis=-1, keepdims=True) + EPS) * nw * _silu(z_ref[n, :, hs])
            o_ref[n, :, hs] = on.astype(o_ref.dtype)

    @pl.when(ci == pl.num_programs(0) - 1)
    def _():
        sout_ref[...] = s_ref[...]


def delta_rec(u, w, at, qg, kd, egl, z_arr, z_colblk, s0, norm_w, n_seq, seq):
    nc = seq // CHUNK
    v3 = lambda x: x.reshape(n_seq, seq, x.shape[-1])
    row = pl.BlockSpec((n_seq, CHUNK, C_WIDTH), lambda t: (0, t, 0))
    s_spec = pl.BlockSpec((n_seq * C_HEADS, HEAD_DIM, HEAD_DIM), lambda t: (0, 0, 0))
    o, s_fin = pl.pallas_call(
        functools.partial(_delta_rec_kernel, n_seq=n_seq),
        out_shape=[jax.ShapeDtypeStruct((n_seq, seq, C_WIDTH), BF16),
                   jax.ShapeDtypeStruct((n_seq * C_HEADS, HEAD_DIM, HEAD_DIM), F32)],
        grid=(nc,),
        in_specs=[row, row, pl.BlockSpec((n_seq, CHUNK, C_HEADS * CHUNK), lambda t: (0, t, 0)), row, row,
                  pl.BlockSpec((n_seq, 1, 8, C_WIDTH), lambda t: (0, t, 0, 0)),
                  pl.BlockSpec((n_seq, CHUNK, C_WIDTH), lambda t: (0, t, z_colblk)),
                  s_spec, pl.BlockSpec((1, HEAD_DIM), lambda t: (0, 0))],
        out_specs=[row, s_spec],
        scratch_shapes=[pltpu.VMEM((n_seq * C_HEADS, HEAD_DIM, HEAD_DIM), F32)],
        compiler_params=_cp("arbitrary"),
        name="delta_rec",
    )(v3(u), v3(w), v3(at), v3(qg), v3(kd), egl.reshape(n_seq, nc, 8, C_WIDTH), v3(z_arr),
      s0.reshape(n_seq * C_HEADS, HEAD_DIM, HEAD_DIM), norm_w.reshape(1, HEAD_DIM))
    return o.reshape(n_seq * seq, C_WIDTH), s_fin.reshape(n_seq, C_HEADS, HEAD_DIM, HEAD_DIM)


def _delta_chunk_kernel(q_ref, k_ref, v_ref, bb_ref, gb_ref, z_ref, s0_ref, nw_ref, o_ref, sout_ref, s_ref, *, c):
    ci = pl.program_id(1)

    @pl.when(ci == 0)
    def _():
        s_ref[...] = s0_ref[0]

    ii = lax.broadcasted_iota(jnp.int32, (c, c), 0)
    jj = lax.broadcasted_iota(jnp.int32, (c, c), 1)
    tri = (ii >= jj).astype(F32)
    sel0 = (lax.broadcasted_iota(jnp.int32, (c, HEAD_DIM), 1) == 0).astype(F32)
    nw = nw_ref[...]
    for h in range(C_HEADS):
        lo = h * HEAD_DIM
        q = q_ref[:, lo:lo + HEAD_DIM]
        k = k_ref[:, lo:lo + HEAD_DIM]
        v = v_ref[:, lo:lo + HEAD_DIM]
        bb = bb_ref[:, lo:lo + HEAD_DIM]
        gcb = _dot(tri, gb_ref[:, lo:lo + HEAD_DIM], HI)
        gcr = _dot_nt(sel0, gcb, HI)
        dec = jnp.where(ii >= jj, jnp.exp(jnp.minimum(gcb[:, :c] - gcr, 0.0)), 0.0)
        kb = k * bb
        low = jnp.where(ii > jj, _dot_nt(kb, k, HI) * dec, 0.0)
        tmat = _unit_lower_inverse(low, c)
        u = _dot(tmat, v * bb, HI)
        w = _dot(tmat, kb * jnp.exp(gcb), HI)
        s = s_ref[h]
        v_new = u - _dot(w, s, HI)
        attn = _dot_nt(q, k, HI) * dec
        o = _dot(q * jnp.exp(gcb), s, HI) + _dot(attn, v_new, HI)
        g_last = gcb[c - 1:c, :]
        s_ref[h] = s * jnp.exp(g_last) + _dot_tn(k * jnp.exp(g_last - gcb), v_new, HI)
        on = o * lax.rsqrt(jnp.mean(o * o, axis=-1, keepdims=True) + EPS) * nw * _silu(z_ref[:, lo:lo + HEAD_DIM])
        o_ref[:, lo:lo + HEAD_DIM] = on.astype(o_ref.dtype)

    @pl.when(ci == pl.num_programs(1) - 1)
    def _():
        sout_ref[0] = s_ref[...]


def delta_chunks(q, k, v, bb, gb, z_arr, z_colblk, s0, s0_off, norm_w, n_seq, c, out_dtype):
    m = q.shape[0]
    nc = m // n_seq // c
    row = pl.BlockSpec((c, C_WIDTH), lambda n, t: (n * nc + t, 0))
    s_spec = pl.BlockSpec((1, C_HEADS, HEAD_DIM, HEAD_DIM), lambda n, t: (n, 0, 0, 0))
    s0_spec = pl.BlockSpec((1, C_HEADS, HEAD_DIM, HEAD_DIM), lambda n, t: (s0_off + n, 0, 0, 0))
    return pl.pallas_call(
        functools.partial(_delta_chunk_kernel, c=c),
        out_shape=[jax.ShapeDtypeStruct((m, C_WIDTH), out_dtype),
                   jax.ShapeDtypeStruct((n_seq, C_HEADS, HEAD_DIM, HEAD_DIM), F32)],
        grid=(n_seq, nc),
        in_specs=[row, row, row, row, row,
                  pl.BlockSpec((c, C_WIDTH), lambda n, t: (n * nc + t, z_colblk)),
                  s0_spec, pl.BlockSpec((1, HEAD_DIM), lambda n, t: (0, 0))],
        out_specs=[row, s_spec],
        scratch_shapes=[pltpu.VMEM((C_HEADS, HEAD_DIM, HEAD_DIM), F32)],
        compiler_params=_cp("parallel", "arbitrary"),
        name="delta_chunks",
    )(q, k, v, bb, gb, z_arr, s0, norm_w.reshape(1, HEAD_DIM))


def _merge_kernel(oa_ref, ob_ref, oc_ref, ga_ref, gb_ref, gc_ref, wa_ref, wb_ref, wc_ref, o_ref, *, prec):
    ya = _dot(oa_ref[...], wa_ref[...], prec)
    yb = _dot(ob_ref[...], wb_ref[...], prec)
    yc = _dot(oc_ref[...], wc_ref[...], prec)
    t = _sigmoid(ga_ref[...]) * ya + _sigmoid(gb_ref[...]) * yb + _sigmoid(gc_ref[...]) * yc
    o_ref[...] = t.astype(o_ref.dtype)


def merge_branches(oa, ob, oc, p, w_a, w_b, w_c, tm, tn):
    m = oa.shape[0]
    gate0 = 10240 // tn
    per = D_MODEL // tn
    o_spec = pl.BlockSpec((tm, 1024), lambda i, j: (i, 0))
    w_spec = pl.BlockSpec((1024, tn), lambda i, j: (0, j))
    gate_spec = lambda b: pl.BlockSpec((tm, tn), lambda i, j: (i, gate0 + b * per + j))
    return pl.pallas_call(
        functools.partial(_merge_kernel, prec=_prec(w_a)),
        out_shape=jax.ShapeDtypeStruct((m, D_MODEL), w_a.dtype),
        grid=(m // tm, D_MODEL // tn),
        in_specs=[o_spec, o_spec, o_spec, gate_spec(0), gate_spec(1), gate_spec(2), w_spec, w_spec, w_spec],
        out_specs=pl.BlockSpec((tm, tn), lambda i, j: (i, j)),
        compiler_params=_cp("parallel", "arbitrary"),
        name="merge_branches",
    )(oa, ob, oc, p, p, p, w_a, w_b, w_c)


def _router_kernel(x_ref, g_ref, r_ref, h_ref, sel_ref):
    x = x_ref[...]
    hn = x * lax.rsqrt(jnp.mean(x * x, axis=-1, keepdims=True) + EPS) * g_ref[...]
    h_ref[...] = hn
    logits = _dot(hn, r_ref[...], HI)
    lane_i = lax.broadcasted_iota(jnp.int32, logits.shape, 1)
    lane = lane_i.astype(F32)
    logits = jnp.where(lane_i < N_EXPERTS, logits, NEG)
    v1 = jnp.max(logits, axis=-1, keepdims=True)
    i1 = jnp.min(jnp.where(logits == v1, lane, 128.0), axis=-1, keepdims=True)
    rest = jnp.where(lane == i1, NEG, logits)
    v2 = jnp.max(rest, axis=-1, keepdims=True)
    i2 = jnp.min(jnp.where(rest == v2, lane, 128.0), axis=-1, keepdims=True)
    e = jnp.exp(v2 - v1)
    tot = 1.0 + e
    sel_ref[...] = jnp.where(lane_i == 0, i1, jnp.where(lane_i == 1, i2, jnp.where(lane_i == 2, 1.0 / tot, e / tot)))


def router(x, g, router_w, tm):
    m, d = x.shape
    rw = jnp.zeros((d, 128), F32).at[:, :N_EXPERTS].set(router_w)
    return pl.pallas_call(
        _router_kernel,
        out_shape=[jax.ShapeDtypeStruct((m, d), F32), jax.ShapeDtypeStruct((m, 128), F32)],
        grid=(m // tm,),
        in_specs=[pl.BlockSpec((tm, d), lambda i: (i, 0)), pl.BlockSpec((1, d), lambda i: (0, 0)),
                  pl.BlockSpec((d, 128), lambda i: (0, 0))],
        out_specs=[pl.BlockSpec((tm, d), lambda i: (i, 0)), pl.BlockSpec((tm, 128), lambda i: (i, 0))],
        compiler_params=_cp("parallel"),
        name="router",
    )(x, g.reshape(1, d), rw)


MOE_TM = 256


def _moe_plan(sel, m):
    i1 = sel[:, 0].astype(jnp.int32)
    i2 = sel[:, 1].astype(jnp.int32)
    e_flat = jnp.concatenate([i1, i2])
    tok = jnp.tile(jnp.arange(m, dtype=jnp.int32), 2)
    onehot = (e_flat[:, None] == jnp.arange(N_EXPERTS, dtype=jnp.int32)[None, :]).astype(jnp.int32)
    counts = jnp.sum(onehot, axis=0)
    rank = jnp.take_along_axis(jnp.cumsum(onehot, axis=0) - onehot, e_flat[:, None], axis=1)[:, 0]
    padded = (counts + MOE_TM - 1) // MOE_TM * MOE_TM
    ends = jnp.cumsum(padded)
    pos = (ends - padded)[e_flat] + rank
    n_tiles = (2 * m) // MOE_TM + N_EXPERTS
    row_token = jnp.zeros((n_tiles * MOE_TM,), jnp.int32).at[pos].set(tok)
    tiles = jnp.arange(n_tiles, dtype=jnp.int32)
    n_used = ends[-1] // MOE_TM
    tile_src = jnp.minimum(tiles, n_used - 1)
    tile_expert = jnp.sum((tile_src * MOE_TM)[:, None] >= ends[None, :], axis=1).astype(jnp.int32)
    tile_used = (tiles < n_used).astype(jnp.int32)
    return row_token, pos[:m], pos[m:], tile_expert, tile_src, tile_used


def _moe_gather_kernel(tok_ref, used_ref, h_hbm, o_ref, buf, sem):
    i = pl.program_id(0)

    @pl.when(used_ref[i] == 1)
    def _():
        def issue(r, carry):
            pltpu.make_async_copy(h_hbm.at[pl.ds(tok_ref[i * MOE_TM + r], 1)], buf.at[pl.ds(r, 1)], sem).start()
            return carry
        lax.fori_loop(0, MOE_TM, issue, 0)
        pltpu.make_async_copy(h_hbm.at[pl.ds(0, MOE_TM)], buf, sem).wait()
        o_ref[...] = buf[...].astype(o_ref.dtype)

    @pl.when(used_ref[i] == 0)
    def _():
        o_ref[...] = jnp.zeros_like(o_ref)


def moe_gather(h, row_token, tile_used):
    d = h.shape[1]
    n_tiles = tile_used.shape[0]
    return pl.pallas_call(
        _moe_gather_kernel,
        out_shape=jax.ShapeDtypeStruct((n_tiles * MOE_TM, d), BF16),
        grid_spec=pltpu.PrefetchScalarGridSpec(
            num_scalar_prefetch=2,
            grid=(n_tiles,),
            in_specs=[pl.BlockSpec(memory_space=pl.ANY)],
            out_specs=pl.BlockSpec((MOE_TM, d), lambda i, tok, used: (i, 0)),
            scratch_shapes=[pltpu.VMEM((MOE_TM, d), F32), pltpu.SemaphoreType.DMA(())]),
        compiler_params=_cp("arbitrary"),
        name="moe_gather",
    )(row_token, tile_used, h)


def _moe_up_kernel(te_ref, ts_ref, tu_ref, h_ref, wg_ref, wu_ref, o_ref, wg_bf, wu_bf):
    t = pl.program_id(1)
    used = tu_ref[t] == 1
    new_block = jnp.logical_or(t == 0, te_ref[t] != te_ref[jnp.maximum(t - 1, 0)])

    @pl.when(jnp.logical_and(used, new_block))
    def _():
        wg_bf[...] = wg_ref[0].astype(BF16)
        wu_bf[...] = wu_ref[0].astype(BF16)

    @pl.when(used)
    def _():
        h = h_ref[...]
        o_ref[...] = (_silu(_dot(h, wg_bf[...])) * _dot(h, wu_bf[...])).astype(o_ref.dtype)

    @pl.when(jnp.logical_not(used))
    def _():
        o_ref[...] = jnp.zeros_like(o_ref)


def moe_up(hs, wg, wu, plan, tn):
    tile_expert, tile_src, tile_used = plan
    d, f = wg.shape[1], wg.shape[2]
    n_tiles = tile_used.shape[0]
    w_spec = pl.BlockSpec((1, d, tn), lambda j, t, te, ts, tu: (te[t], 0, j))
    return pl.pallas_call(
        _moe_up_kernel,
        out_shape=jax.ShapeDtypeStruct((n_tiles * MOE_TM, f), BF16),
        grid_spec=pltpu.PrefetchScalarGridSpec(
            num_scalar_prefetch=3,
            grid=(f // tn, n_tiles),
            in_specs=[pl.BlockSpec((MOE_TM, d), lambda j, t, te, ts, tu: (ts[t], 0)), w_spec, w_spec],
            out_specs=pl.BlockSpec((MOE_TM, tn), lambda j, t, te, ts, tu: (t, j)),
            scratch_shapes=[pltpu.VMEM((d, tn), BF16), pltpu.VMEM((d, tn), BF16)]),
        compiler_params=_cp("arbitrary", "arbitrary"),
        name="moe_up",
    )(tile_expert, tile_src, tile_used, hs, wg, wu)


def _moe_down_kernel(te_ref, ts_ref, tu_ref, a_ref, w_ref, o_ref):
    used = tu_ref[pl.program_id(1)] == 1

    @pl.when(used)
    def _():
        o_ref[...] = _dot(a_ref[...], w_ref[0])

    @pl.when(jnp.logical_not(used))
    def _():
        o_ref[...] = jnp.zeros_like(o_ref)


def moe_down(act, wd, plan, tn):
    tile_expert, tile_src, tile_used = plan
    f, d = wd.shape[1], wd.shape[2]
    n_tiles = tile_used.shape[0]
    return pl.pallas_call(
        _moe_down_kernel,
        out_shape=jax.ShapeDtypeStruct((n_tiles * MOE_TM, d), F32),
        grid_spec=pltpu.PrefetchScalarGridSpec(
            num_scalar_prefetch=3,
            grid=(d // tn, n_tiles),
            in_specs=[pl.BlockSpec((MOE_TM, f), lambda j, t, te, ts, tu: (ts[t], 0)),
                      pl.BlockSpec((1, f, tn), lambda j, t, te, ts, tu: (te[t], 0, j))],
            out_specs=pl.BlockSpec((MOE_TM, tn), lambda j, t, te, ts, tu: (t, j))),
        compiler_params=_cp("arbitrary", "arbitrary"),
        name="moe_down",
    )(tile_expert, tile_src, tile_used, act, wd)


def _moe_combine_kernel(p1_ref, p2_ref, x_ref, sel_ref, y_hbm, o_ref, b1, b2, sems, *, tm):
    base = pl.program_id(0) * tm

    def issue(r, carry):
        pltpu.make_async_copy(y_hbm.at[pl.ds(p1_ref[base + r], 1)], b1.at[pl.ds(r, 1)], sems.at[0]).start()
        pltpu.make_async_copy(y_hbm.at[pl.ds(p2_ref[base + r], 1)], b2.at[pl.ds(r, 1)], sems.at[1]).start()
        return carry
    lax.fori_loop(0, tm, issue, 0)
    pltpu.make_async_copy(y_hbm.at[pl.ds(0, tm)], b1, sems.at[0]).wait()
    pltpu.make_async_copy(y_hbm.at[pl.ds(0, tm)], b2, sems.at[1]).wait()
    o_ref[...] = x_ref[...] + (sel_ref[:, 2:3] * b1[...] + sel_ref[:, 3:4] * b2[...])


def moe_combine(x, sel, y, pos1, pos2, tm):
    m, d = x.shape
    return pl.pallas_call(
        functools.partial(_moe_combine_kernel, tm=tm),
        out_shape=jax.ShapeDtypeStruct((m, d), F32),
        grid_spec=pltpu.PrefetchScalarGridSpec(
            num_scalar_prefetch=2,
            grid=(m // tm,),
            in_specs=[pl.BlockSpec((tm, d), lambda i, p1, p2: (i, 0)),
                      pl.BlockSpec((tm, 128), lambda i, p1, p2: (i, 0)),
                      pl.BlockSpec(memory_space=pl.ANY)],
            out_specs=pl.BlockSpec((tm, d), lambda i, p1, p2: (i, 0)),
            scratch_shapes=[pltpu.VMEM((tm, d), F32), pltpu.VMEM((tm, d), F32), pltpu.SemaphoreType.DMA((2,))]),
        compiler_params=_cp("arbitrary"),
        name="moe_combine",
    )(pos1, pos2, x, sel, y)


def moe_ffn(x, g, router_w, wg, wu, wd):
    m = x.shape[0]
    h, sel = router(x, g, router_w, 320)
    row_token, pos1, pos2, tile_expert, tile_src, tile_used = _moe_plan(sel, m)
    plan = (tile_expert, tile_src, tile_used)
    hs = moe_gather(h, row_token, tile_used)
    act = moe_up(hs, wg, wu, plan, 1024)
    y = moe_down(act, wd, plan, 512)
    return moe_combine(x, sel, y, pos1, pos2, 128)


def _split_w_in(w):
    qa, ka, va, qb, kb, vb, cqkv, z, cb, ca, gates = jnp.split(w, np.cumsum(PROJ_SIZES)[:-1].tolist(), axis=-1)
    main = jnp.concatenate([qa, qb, ka, va, kb, vb, cqkv, z, gates], axis=-1)
    small = jnp.concatenate([cb, ca, jnp.zeros((w.shape[0], 112), w.dtype)], axis=-1)
    return main, small


def _mix_prompt(h, lp):
    w_main, w_small, sink, conv_w, a_log, dt_bias, norm_w, w_a, w_b, w_c = lp
    m = h.shape[0]
    p = matmul(h, w_main, 2048, 512)
    pc = matmul(h, w_small, 512, 128)
    p3 = p.reshape(BATCH, SEQ, PW)
    oa, = band_attention(p3, 1, 0, 16, 17, _alibi(A_HEADS), sink, False, BF16)
    slopes_b = _alibi(N_BGROUPS * B_HPG).reshape(N_BGROUPS, B_HPG)
    outs, lses = [], []
    for gi, (_, dil) in enumerate(B_GROUPS):
        o, lse = band_attention(p3, dil, 1 + gi, 18 + gi, 21 + gi, slopes_b[gi], None, True, F32)
        outs.append(o)
        lses.append(lse)
    ob = combine_groups(outs, lses, 512, BF16)
    tm = 256
    q, k, v, bb, gb = delta_prep(p, 2, p, 2, lambda i: jnp.maximum(i * (tm // 8) - 1, 0), pc, conv_w, a_log,
                                 dt_bias, tm, SEQ // tm, tm)
    s0 = jnp.zeros((BATCH, C_HEADS, HEAD_DIM, HEAD_DIM), F32)
    u, w, at, qg, kd, egl = delta_wy(q, k, v, bb, gb, 2)
    oc, s_fin = delta_rec(u, w, at, qg, kd, egl, p, 9, s0, norm_w, BATCH, SEQ)
    t = merge_branches(oa, ob, oc, p, w_a, w_b, w_c, 512, 512)
    p4 = p.reshape(BATCH, SEQ, PW)
    kv = lambda blk, win: p4[:, SEQ - win:, blk * 256:(blk + 1) * 256].reshape(BATCH, win, 2, HEAD_DIM)
    state = [kv(16, A_WINDOW), kv(17, A_WINDOW)]
    for gi, (win, _) in enumerate(B_GROUPS):
        state += [kv(18 + gi, win), kv(21 + gi, win)]
    state += [p4[:, SEQ - (CONV_K - 1):, 6144:9216], s_fin]
    return t, state


def _mix_sample(h, lp, caches, layer):
    w_main, w_small, sink, conv_w, a_log, dt_bias, norm_w, w_a, w_b, w_c = lp
    c_ak, c_av, b1k, b1v, b2k, b2v, b3k, b3v, conv_all, s0_all = caches
    nb, nt = DEC_BATCH, DEC_SEQ
    m = nb * nt
    p = matmul(h, w_main, m, 512)
    pc = matmul(h, w_small, m, 128)
    p3 = p.reshape(nb, nt, PW)

    def heads_q(blk):
        qq = p3[:, :, blk * 1024:(blk + 1) * 1024].reshape(nb, nt, 2, 4, HEAD_DIM)
        return jnp.transpose(qq, (0, 2, 3, 1, 4)).reshape(nb, 2, 16, HEAD_DIM)

    def heads_o(o):
        oo = o.reshape(nb, 2, 4, nt, HEAD_DIM)
        return jnp.transpose(oo, (0, 3, 1, 2, 4)).reshape(m, 1024)

    def new_rows(blk):
        x = p3[:, :, blk * 256:(blk + 1) * 256]
        return x.reshape(nb, nt, 2, HEAD_DIM), jnp.pad(x, ((0, 0), (0, DEC_PAD - nt), (0, 0)))

    def reach(buf, dil):
        win, keep = buf.shape[2], min(dil, nt)
        x = buf.reshape(DEPTH * nb, win // dil, dil, 256)[:, :, :keep]
        return x.reshape(DEPTH * nb, win // dil * keep, 256), win

    ka, ka_p = new_rows(16)
    va, va_p = new_rows(17)
    oa, _ = window_decode(heads_q(0), reach(c_ak, 1)[0], reach(c_av, 1)[0], layer, A_WINDOW, ka_p, va_p,
                          _alibi(A_HEADS), 1, sink)
    oa = heads_o(oa)
    appended = [ka, va]
    slopes_b = _alibi(N_BGROUPS * B_HPG).reshape(N_BGROUPS, B_HPG)
    outs, lses = [], []
    for gi, ((_, dil), (bk, bv)) in enumerate(zip(B_GROUPS, ((b1k, b1v), (b2k, b2v), (b3k, b3v)))):
        kn, kn_p = new_rows(18 + gi)
        vn, vn_p = new_rows(21 + gi)
        (rk, win), (rv, _) = reach(bk, dil), reach(bv, dil)
        o, lse = window_decode(heads_q(1 + gi), rk, rv, layer, win, kn_p, vn_p, slopes_b[gi], dil, None)
        outs.append(heads_o(o))
        lses.append(heads_o(lse))
        appended += [kn, vn]
    ob = combine_groups(outs, lses, m, F32)

    pad_rows = lambda x: jnp.pad(x.reshape(nb, nt, -1), ((0, 0), (0, DEC_PAD - nt), (0, 0))).reshape(nb * DEC_PAD, -1)
    cq = p3[:, :, 6144:9216]
    xs = pad_rows(cq)
    xprev = jnp.pad(conv_all[layer], ((0, 0), (8 - (CONV_K - 1), 0), (0, 0))).reshape(nb * 8, 3 * C_WIDTH)
    q, k, v, bb, gb = delta_prep(xs, 0, xprev, 0, lambda i: i, pad_rows(pc), conv_w, a_log, dt_bias,
                                 DEC_PAD, None, nt)
    zs = pad_rows(p3[:, :, 9216:10240])
    s0_flat = s0_all.reshape(DEPTH * nb, C_HEADS, HEAD_DIM, HEAD_DIM)
    oc, s_fin = delta_chunks(q, k, v, bb, gb, zs, 0, s0_flat, layer * nb, norm_w, nb, DEC_PAD, F32)
    oc = oc.reshape(nb, DEC_PAD, C_WIDTH)[:, :nt].reshape(m, C_WIDTH)
    t = merge_branches(oa, ob, oc, p, w_a, w_b, w_c, m, 512)
    appended += [cq, s_fin]
    return t, appended


def kernel(x_prompt, x_sample, cache_a_k, cache_a_v, cache_b1_k, cache_b1_v, cache_b2_k, cache_b2_v, cache_b3_k, cache_b3_v, state_c_conv, state_c_rec, norm_mix, norm_ffn, norm_final, w_in, attn_sink, conv_w, a_log, dt_bias, norm_delta, w_out_a, w_out_b, w_out_c, w_out, ffn_w_gate, ffn_w_up, ffn_w_down, router_w, moe_w_gate, moe_w_up, moe_w_down):
    mp, ms = BATCH * SEQ, DEC_BATCH * DEC_SEQ
    xp = x_prompt.reshape(mp, D_MODEL)
    xs = x_sample.reshape(ms, D_MODEL)
    tmp = 512
    p_states, s_states = [], []
    caches = (cache_a_k, cache_a_v, cache_b1_k, cache_b1_v, cache_b2_k, cache_b2_v, cache_b3_k, cache_b3_v,
              state_c_conv, state_c_rec)
    for l in range(DEPTH):
        w_main, w_small = _split_w_in(w_in[l])
        shared = (attn_sink[l], conv_w[l], a_log[l], dt_bias[l], norm_delta[l])
        lp_s = (w_main, w_small, *shared, w_out_a[l], w_out_b[l], w_out_c[l])
        lp_p = (w_main.astype(BF16), w_small.astype(BF16), *shared,
                w_out_a[l].astype(BF16), w_out_b[l].astype(BF16), w_out_c[l].astype(BF16))
        tp, sp = _mix_prompt(rmsnorm(xp, norm_mix[l], BF16, tmp), lp_p)
        ts, ss = _mix_sample(rmsnorm(xs, norm_mix[l], F32, ms), lp_s, caches, l)
        xp = matmul_residual(xp, tp, w_out[l].astype(BF16), tmp, 512)
        xs = matmul_residual(xs, ts, w_out[l], ms, 512)
        p_states.append(sp)
        s_states.append(ss)
        i = l // 2
        if l % 2 == 0:
            wg, wu, wd = ffn_w_gate[i], ffn_w_up[i], ffn_w_down[i]
            hp = rmsnorm(xp, norm_ffn[l], BF16, tmp)
            hs = rmsnorm(xs, norm_ffn[l], F32, ms)
            xp = matmul_residual(xp, swiglu_up(hp, wg.astype(BF16), wu.astype(BF16), tmp, 512), wd.astype(BF16),
                                 tmp, 512)
            xs = matmul_residual(xs, swiglu_up(hs, wg, wu, ms, 512), wd, ms, 512)
        else:
            x_all = moe_ffn(jnp.concatenate([xp, xs], axis=0), norm_ffn[l], router_w[i], moe_w_gate[i],
                            moe_w_up[i], moe_w_down[i].astype(BF16))
            xp, xs = x_all[:mp], x_all[mp:]
    y_prompt = rmsnorm(xp, norm_final, F32, tmp).reshape(BATCH, SEQ, D_MODEL)
    y_sample = rmsnorm(xs, norm_final, F32, ms).reshape(DEC_BATCH, DEC_SEQ, D_MODEL)
    p_out = [jnp.stack(z) for z in zip(*p_states)]
    s_new = [jnp.stack(z) for z in zip(*s_states)]
    s_out = [jnp.concatenate([old, new], axis=2)[:, :, DEC_SEQ:] for old, new in zip(caches[:-1], s_new[:-1])]
    s_out.append(s_new[-1])
    return (y_prompt, y_sample, *p_out, *s_out)
```

```python
import functools

import numpy as np
import jax
import jax.numpy as jnp
from jax import lax
from jax.experimental import pallas as pl
from jax.experimental.pallas import tpu as pltpu

F32 = jnp.float32
BF16 = jnp.bfloat16
HI = lax.Precision.HIGHEST

D_MODEL = 2048
BATCH = 4
SEQ = 2048
DEPTH = 2
DEC_BATCH = 32
DEC_SEQ = 4
HEAD_DIM = 128
A_HEADS = 8
A_KV = 2
A_WINDOW = 128
B_GROUPS = ((128, 1), (512, 4), (2048, 16))
N_BGROUPS = 3
B_HPG = 8
B_KV = 2
C_HEADS = 8
C_WIDTH = C_HEADS * HEAD_DIM
CONV_K = 4
CHUNK = 64
BAND = 128
D_FF = 5632
N_EXPERTS = 8
D_FF_EXPERT = 7168
PROJ_SIZES = (1024, 256, 256, 3072, 768, 768, 3072, 1024, 8, 8, 6144)
EPS = 1e-6
NEG = -1e30

PW = 16384
DEC_PAD = 8
VMEM_LIMIT_BYTES = 56 * 1024 * 1024


def _cp(*sem):
    return pltpu.CompilerParams(dimension_semantics=sem, vmem_limit_bytes=VMEM_LIMIT_BYTES)


def _alibi(n):
    return np.asarray(2.0 ** (-8.0 * np.arange(1, n + 1) / n), dtype=np.float32)


def _sigmoid(x):
    return 1.0 / (1.0 + jnp.exp(-x))


def _silu(x):
    return x * _sigmoid(x)


def _softplus(x):
    return jnp.maximum(x, 0.0) + jnp.log(1.0 + jnp.exp(-jnp.abs(x)))


def _dot(a, b, prec=None):
    return jnp.dot(a, b, precision=prec, preferred_element_type=F32)


def _dot_nt(a, b, prec=None):
    return lax.dot_general(a, b, (((1,), (1,)), ((), ())), precision=prec, preferred_element_type=F32)


def _dot_tn(a, b, prec=None):
    return lax.dot_general(a, b, (((0,), (0,)), ((), ())), precision=prec, preferred_element_type=F32)


def _rmsnorm_kernel(x_ref, g_ref, o_ref):
    x = x_ref[...]
    ms = jnp.mean(x * x, axis=-1, keepdims=True)
    o_ref[...] = (x * lax.rsqrt(ms + EPS) * g_ref[...]).astype(o_ref.dtype)


def rmsnorm(x, g, out_dtype, tm):
    m, d = x.shape
    return pl.pallas_call(
        _rmsnorm_kernel,
        out_shape=jax.ShapeDtypeStruct((m, d), out_dtype),
        grid=(m // tm,),
        in_specs=[pl.BlockSpec((tm, d), lambda i: (i, 0)), pl.BlockSpec((1, d), lambda i: (0, 0))],
        out_specs=pl.BlockSpec((tm, d), lambda i: (i, 0)),
        compiler_params=_cp("parallel"),
        name="rmsnorm",
    )(x, g.reshape(1, d))


def _prec(x):
    return HI if x.dtype == F32 else None


def _mm_kernel(a_ref, w_ref, o_ref, *, prec):
    o_ref[...] = _dot(a_ref[...], w_ref[...], prec).astype(o_ref.dtype)


def matmul(a, w, tm, tn, out_dtype=F32):
    m, k = a.shape
    n = w.shape[1]
    return pl.pallas_call(
        functools.partial(_mm_kernel, prec=_prec(w)),
        out_shape=jax.ShapeDtypeStruct((m, n), out_dtype),
        grid=(m // tm, n // tn),
        in_specs=[pl.BlockSpec((tm, k), lambda i, j: (i, 0)), pl.BlockSpec((k, tn), lambda i, j: (0, j))],
        out_specs=pl.BlockSpec((tm, tn), lambda i, j: (i, j)),
        compiler_params=_cp("parallel", "arbitrary"),
        name="matmul",
    )(a, w)


def _mm_res_kernel(x_ref, a_ref, w_ref, o_ref, *, prec):
    o_ref[...] = x_ref[...] + _dot(a_ref[...], w_ref[...], prec)


def matmul_residual(x, a, w, tm, tn):
    m, k = a.shape
    n = w.shape[1]
    return pl.pallas_call(
        functools.partial(_mm_res_kernel, prec=_prec(w)),
        out_shape=jax.ShapeDtypeStruct((m, n), F32),
        grid=(m // tm, n // tn),
        in_specs=[pl.BlockSpec((tm, tn), lambda i, j: (i, j)),
                  pl.BlockSpec((tm, k), lambda i, j: (i, 0)),
                  pl.BlockSpec((k, tn), lambda i, j: (0, j))],
        out_specs=pl.BlockSpec((tm, tn), lambda i, j: (i, j)),
        compiler_params=_cp("parallel", "arbitrary"),
        name="matmul_residual",
    )(x, a, w)


def _swiglu_kernel(h_ref, wg_ref, wu_ref, o_ref, *, prec):
    h = h_ref[...]
    a = _dot(h, wg_ref[...], prec)
    b = _dot(h, wu_ref[...], prec)
    o_ref[...] = (_silu(a) * b).astype(o_ref.dtype)


def swiglu_up(h, wg, wu, tm, tn):
    m, k = h.shape
    n = wg.shape[1]
    return pl.pallas_call(
        functools.partial(_swiglu_kernel, prec=_prec(wg)),
        out_shape=jax.ShapeDtypeStruct((m, n), wg.dtype),
        grid=(m // tm, n // tn),
        in_specs=[pl.BlockSpec((tm, k), lambda i, j: (i, 0)),
                  pl.BlockSpec((k, tn), lambda i, j: (0, j)),
                  pl.BlockSpec((k, tn), lambda i, j: (0, j))],
        out_specs=pl.BlockSpec((tm, tn), lambda i, j: (i, j)),
        compiler_params=_cp("parallel", "arbitrary"),
        name="swiglu_up",
    )(h, wg, wu)


def _band_kernel(slope_ref, sink_ref, q_ref, k_ref, v_ref, o_ref, *lse_refs, dil, has_sink):
    h = pl.program_id(1)
    seq = q_ref.shape[1]
    span = BAND * dil
    qi = lax.broadcasted_iota(jnp.int32, (BAND, 2 * BAND), 0)
    ki = lax.broadcasted_iota(jnp.int32, (BAND, 2 * BAND), 1)
    dist = qi + BAND - ki
    band = (dist >= 0) & (dist <= BAND)
    distf = dist.astype(F32) * float(dil)
    scale = HEAD_DIM ** -0.5

    def block(it, carry):
        r = it % dil
        b = it // dil
        cur = pl.ds(b * span + r, BAND, stride=dil)
        prv = pl.ds(jnp.maximum(b - 1, 0) * span + r, BAND, stride=dil)
        k = jnp.concatenate([k_ref[0, prv, :], k_ref[0, cur, :]], axis=0).astype(BF16)
        v = jnp.concatenate([v_ref[0, prv, :], v_ref[0, cur, :]], axis=0).astype(BF16)
        mask = band & ((ki >= BAND) | (b > 0))
        q = (q_ref[0, cur, :] * scale).astype(BF16)
        s = jnp.where(mask, _dot_nt(q, k) - slope_ref[h] * distf, NEG)
        m = jnp.max(s, axis=-1, keepdims=True)
        if has_sink:
            sink = sink_ref[h]
            m = jnp.maximum(m, sink)
        e = jnp.exp(s - m)
        den = jnp.sum(e, axis=-1, keepdims=True)
        if has_sink:
            den = den + jnp.exp(sink - m)
        o_ref[0, cur, :] = (_dot(e.astype(BF16), v) / den).astype(o_ref.dtype)
        for lse_ref in lse_refs:
            lse_ref[0, cur, :] = jnp.broadcast_to(m + jnp.log(den), (BAND, HEAD_DIM))
        return carry

    lax.fori_loop(0, seq // BAND, block, 0, unroll=8)


def band_attention(p3, dil, qblk, kblk, vblk, slopes, sink, want_lse, out_dtype):
    n_seq, seq, _ = p3.shape
    has_sink = sink is not None
    smem = pl.BlockSpec(memory_space=pltpu.SMEM)
    kv_spec = lambda blk: pl.BlockSpec((1, seq, HEAD_DIM), lambda n, h: (n, 0, 2 * blk + h // 4))
    o_spec = pl.BlockSpec((1, seq, HEAD_DIM), lambda n, h: (n, 0, h))
    out_shape = [jax.ShapeDtypeStruct((n_seq, seq, 1024), out_dtype)]
    if want_lse:
        out_shape.append(jax.ShapeDtypeStruct((n_seq, seq, 1024), F32))
    outs = pl.pallas_call(
        functools.partial(_band_kernel, dil=dil, has_sink=has_sink),
        out_shape=out_shape,
        grid=(n_seq, 8),
        in_specs=[smem, smem, pl.BlockSpec((1, seq, HEAD_DIM), lambda n, h: (n, 0, 8 * qblk + h)),
                  kv_spec(kblk), kv_spec(vblk)],
        out_specs=[o_spec] * len(out_shape),
        compiler_params=_cp("parallel", "arbitrary"),
        name="band_attention",
    )(jnp.asarray(slopes, F32), sink if has_sink else jnp.zeros((A_HEADS,), F32), p3, p3, p3)
    return [o.reshape(n_seq * seq, 1024) for o in outs]


def _decode_kernel(*refs, slopes, dil, win, has_sink):
    if has_sink:
        sink_ref, refs = refs[0], refs[1:]
    q_ref, bk_ref, bv_ref, nk_ref, nv_ref, o_ref, lse_ref = refs
    rows = 4 * DEC_SEQ
    keep = min(dil, DEC_SEQ)
    held = win // dil * keep
    ri = lax.broadcasted_iota(jnp.int32, (rows, held), 0)
    ci = lax.broadcasted_iota(jnp.int32, (rows, held), 1)
    wpos = (ci // keep) * dil + ci % keep
    t1 = win + (ri & 3) - wpos
    valid1 = (wpos >= (ri & 3)) & ((t1 & (dil - 1)) == 0)
    ri2 = lax.broadcasted_iota(jnp.int32, (rows, DEC_PAD), 0)
    ci2 = lax.broadcasted_iota(jnp.int32, (rows, DEC_PAD), 1)
    t2 = (ri2 & 3) - ci2
    valid2 = (t2 >= 0) & ((t2 & (dil - 1)) == 0) & (ci2 < DEC_SEQ)
    hr = lax.broadcasted_iota(jnp.int32, (rows, 1), 0) >> 2
    scale = HEAD_DIM ** -0.5
    for g in range(2):
        lo = g * HEAD_DIM
        slope = jnp.zeros((rows, 1), F32)
        for r in range(4):
            slope = jnp.where(hr == r, float(slopes[g * 4 + r]), slope)
        q = q_ref[0, g] * scale
        s1 = _dot_nt(q, bk_ref[0, :, lo:lo + HEAD_DIM], HI)
        s2 = _dot_nt(q, nk_ref[0, :, lo:lo + HEAD_DIM], HI)
        s1 = jnp.where(valid1, s1 - slope * t1.astype(F32), NEG)
        s2 = jnp.where(valid2, s2 - slope * t2.astype(F32), NEG)
        m = jnp.maximum(jnp.max(s1, axis=-1, keepdims=True), jnp.max(s2, axis=-1, keepdims=True))
        if has_sink:
            sink = jnp.zeros((rows, 1), F32)
            for r in range(4):
                sink = jnp.where(hr == r, sink_ref[g * 4 + r], sink)
            m = jnp.maximum(m, sink)
        e1 = jnp.exp(s1 - m)
        e2 = jnp.exp(s2 - m)
        den = jnp.sum(e1, axis=-1, keepdims=True) + jnp.sum(e2, axis=-1, keepdims=True)
        if has_sink:
            den = den + jnp.exp(sink - m)
        o = _dot(e1, bv_ref[0, :, lo:lo + HEAD_DIM], HI) + _dot(e2, nv_ref[0, :, lo:lo + HEAD_DIM], HI)
        o_ref[0, g] = o / den
        lse_ref[0, g] = jnp.broadcast_to(m + jnp.log(den), (rows, HEAD_DIM))


def window_decode(q, buf_k, buf_v, layer, win, new_k, new_v, slopes, dil, sink):
    nb, held = q.shape[0], buf_k.shape[1]
    has_sink = sink is not None
    in_specs = [
        pl.BlockSpec((1, 2, 16, HEAD_DIM), lambda b: (b, 0, 0, 0)),
        pl.BlockSpec((1, held, 256), lambda b: (layer * nb + b, 0, 0)),
        pl.BlockSpec((1, held, 256), lambda b: (layer * nb + b, 0, 0)),
        pl.BlockSpec((1, DEC_PAD, 256), lambda b: (b, 0, 0)),
        pl.BlockSpec((1, DEC_PAD, 256), lambda b: (b, 0, 0)),
    ]
    ins = [q, buf_k, buf_v, new_k, new_v]
    if has_sink:
        in_specs = [pl.BlockSpec(memory_space=pltpu.SMEM)] + in_specs
        ins = [sink] + ins
    o_shape = jax.ShapeDtypeStruct((nb, 2, 16, HEAD_DIM), F32)
    o_spec = pl.BlockSpec((1, 2, 16, HEAD_DIM), lambda b: (b, 0, 0, 0))
    return pl.pallas_call(
        functools.partial(_decode_kernel, slopes=tuple(float(s) for s in slopes), dil=dil, win=win,
                          has_sink=has_sink),
        out_shape=[o_shape, o_shape],
        grid=(nb,),
        in_specs=in_specs,
        out_specs=[o_spec, o_spec],
        compiler_params=_cp("parallel"),
        name="window_decode",
    )(*ins)


def _combine_kernel(o1, o2, o3, l1, l2, l3, out_ref):
    a, b, c = l1[...], l2[...], l3[...]
    m = jnp.maximum(jnp.maximum(a, b), c)
    wa, wb, wc = jnp.exp(a - m), jnp.exp(b - m), jnp.exp(c - m)
    tot = wa + wb + wc
    out_ref[...] = ((wa * o1[...] + wb * o2[...] + wc * o3[...]) / tot).astype(out_ref.dtype)


def combine_groups(outs, lses, tm, out_dtype):
    m, d = outs[0].shape
    spec = pl.BlockSpec((tm, d), lambda i: (i, 0))
    return pl.pallas_call(
        _combine_kernel,
        out_shape=jax.ShapeDtypeStruct((m, d), out_dtype),
        grid=(m // tm,),
        in_specs=[spec] * 6,
        out_specs=spec,
        compiler_params=_cp("parallel"),
        name="combine_groups",
    )(*outs, *lses)


def _delta_prep_kernel(x_ref, xp_ref, pc_ref, cw_ref, alog_ref, dtb_ref, eb_ref, eg_ref,
                       q_ref, k_ref, v_ref, bb_ref, gb_ref, *, tm, tiles_per_seq, n_valid):
    i = pl.program_id(0)
    x = x_ref[...]
    prev = xp_ref[...]
    if tiles_per_seq is not None:
        prev = jnp.where(i % tiles_per_seq == 0, 0.0, prev)
    row8 = lax.broadcasted_iota(jnp.int32, (8, 1), 0)
    conv = x * cw_ref[CONV_K - 1:CONV_K, :]
    for back in range(1, CONV_K):
        xr = pltpu.roll(x, back, 0)
        head = jnp.where(row8 < back, pltpu.roll(prev, back, 0), xr[:8])
        sh = head if tm == 8 else jnp.concatenate([head, xr[8:]], axis=0)
        conv = conv + sh * cw_ref[CONV_K - 1 - back:CONV_K - back, :]
    act = _silu(conv)
    pc = pc_ref[...]
    beta = _sigmoid(pc)
    gl = -jnp.exp(alog_ref[...]) * _softplus(pc + dtb_ref[...])
    bb = _dot(beta, eb_ref[...], HI)
    gb = _dot(gl, eg_ref[...], HI)
    if n_valid < tm:
        live = lax.broadcasted_iota(jnp.int32, (tm, 1), 0) < n_valid
        act = jnp.where(live, act, 0.0)
        bb = jnp.where(live, bb, 0.0)
        gb = jnp.where(live, gb, 0.0)
    bb_ref[...] = bb
    gb_ref[...] = gb
    for h in range(C_HEADS):
        lo = h * HEAD_DIM
        qh = act[:, lo:lo + HEAD_DIM]
        kh = act[:, C_WIDTH + lo:C_WIDTH + lo + HEAD_DIM]
        qn = qh * lax.rsqrt(jnp.sum(qh * qh, axis=-1, keepdims=True) + EPS)
        q_ref[:, lo:lo + HEAD_DIM] = qn * (HEAD_DIM ** -0.5)
        k_ref[:, lo:lo + HEAD_DIM] = kh * lax.rsqrt(jnp.sum(kh * kh, axis=-1, keepdims=True) + EPS)
    v_ref[...] = act[:, 2 * C_WIDTH:]


def delta_prep(x_arr, x_colblk, xp_arr, xp_colblk, xp_rowmap, pc, conv_w, a_log, dt_bias, tm, tiles_per_seq, n_valid):
    m = pc.shape[0]
    w3 = 3 * C_WIDTH
    alog = jnp.zeros((1, 128), F32).at[0, 8:16].set(a_log)
    dtb = jnp.zeros((1, 128), F32).at[0, 8:16].set(dt_bias)
    lane_head = np.arange(C_WIDTH) // HEAD_DIM
    eb = jnp.asarray((np.arange(128)[:, None] == lane_head[None, :]).astype(np.float32))
    eg = jnp.asarray((np.arange(128)[:, None] == lane_head[None, :] + 8).astype(np.float32))
    full = lambda shape: pl.BlockSpec(shape, lambda i: (0, 0))
    o_shape = jax.ShapeDtypeStruct((m, C_WIDTH), F32)
    o_spec = pl.BlockSpec((tm, C_WIDTH), lambda i: (i, 0))
    return pl.pallas_call(
        functools.partial(_delta_prep_kernel, tm=tm, tiles_per_seq=tiles_per_seq, n_valid=n_valid),
        out_shape=[o_shape] * 5,
        grid=(m // tm,),
        in_specs=[pl.BlockSpec((tm, w3), lambda i: (i, x_colblk)),
                  pl.BlockSpec((8, w3), lambda i: (xp_rowmap(i), xp_colblk)),
                  pl.BlockSpec((tm, 128), lambda i: (i, 0)),
                  full((CONV_K, w3)), full((1, 128)), full((1, 128)), full((128, C_WIDTH)), full((128, C_WIDTH))],
        out_specs=[o_spec] * 5,
        compiler_params=_cp("parallel"),
        name="delta_prep",
    )(x_arr, xp_arr, pc, conv_w, alog, dtb, eb, eg)


def _dot_hi(a, b):
    return _dot(a, b, HI)


def _unit_lower_inverse(low, c, mm=_dot_hi):
    ii = lax.broadcasted_iota(jnp.int32, (c, c), 0)
    jj = lax.broadcasted_iota(jnp.int32, (c, c), 1)
    eye = (ii == jj).astype(F32)
    ld = jnp.where((ii >> 3) == (jj >> 3), low, 0.0)
    l2 = mm(ld, ld)
    l4 = mm(l2, l2)
    inv = mm(mm(eye - ld, eye + l2), eye + l4)
    s = 3
    while (1 << s) < c:
        off = ((ii >> (s + 1)) == (jj >> (s + 1))) & ((ii >> s) != (jj >> s))
        lo = jnp.where(off, low, 0.0)
        inv = inv - mm(mm(inv, lo), inv)
        s += 1
    return inv


def _split_bf16(x, n):
    parts, rest = [], x
    for _ in range(n):
        part = rest.astype(BF16)
        parts.append(part)
        rest = rest - part.astype(F32)
    return parts


def _dot_exact_lhs(a, b, nt=False):
    mm = _dot_nt if nt else _dot
    hi, mid, lo = _split_bf16(b, 3)
    return mm(a, hi) + (mm(a, mid) + mm(a, lo))


def _bdot(a, b):
    return lax.dot_general(a, b, (((2,), (1,)), ((0,), (0,))), preferred_element_type=F32)


def _bdot_hi(a, b):
    return lax.dot_general(a, b, (((2,), (1,)), ((0,), (0,))), precision=HI, preferred_element_type=F32)


def _bdot_nt(a, b):
    return lax.dot_general(a, b, (((2,), (2,)), ((0,), (0,))), preferred_element_type=F32)


def _bdot3(a, b):
    a_hi, a_lo = _split_bf16(a, 2)
    b_hi, b_lo = _split_bf16(b, 2)
    return _bdot(a_hi, b_hi) + (_bdot(a_hi, b_lo) + _bdot(a_lo, b_hi))


def _delta_wy_kernel(q_ref, k_ref, v_ref, bb_ref, gb_ref, u_ref, w_ref, at_ref, qg_ref, kd_ref, egl_ref, *, nch):
    c = CHUNK
    ii = lax.broadcasted_iota(jnp.int32, (c, c), 0)
    jj = lax.broadcasted_iota(jnp.int32, (c, c), 1)
    tri = (ii >= jj).astype(BF16)
    sel0 = (lax.broadcasted_iota(jnp.int32, (c, HEAD_DIM), 1) == 0).astype(BF16)
    heads = [slice(h * HEAD_DIM, (h + 1) * HEAD_DIM) for h in range(C_HEADS)]
    for t in range(nch):
        rows = slice(t * c, (t + 1) * c)
        q, k, v, bb = q_ref[rows, :], k_ref[rows, :], v_ref[rows, :], bb_ref[rows, :]
        gcb = _dot_exact_lhs(tri, gb_ref[rows, :])
        eg = jnp.exp(gcb)
        g_last = gcb[c - 1:c, :]
        kd_ref[rows, :] = (k * jnp.exp(g_last - gcb)).astype(BF16)
        qg_ref[rows, :] = (q * eg).astype(BF16)
        egl_ref[t] = jnp.broadcast_to(jnp.exp(g_last), (8, C_WIDTH))
        kb = k * bb
        vb = v * bb
        kbe = kb * eg
        gcr = jnp.stack([_dot_exact_lhs(sel0, gcb[:, hs], nt=True) for hs in heads])
        gci = jnp.stack([gcb[:, hs][:, :c] for hs in heads])
        dec = jnp.where(ii >= jj, jnp.exp(jnp.minimum(gci - gcr, 0.0)), 0.0)
        lhs = jnp.stack([jnp.concatenate([kb[:, hs], q[:, hs]], axis=0) for hs in heads]).astype(BF16)
        qk = _bdot_nt(lhs, jnp.stack([k[:, hs] for hs in heads]).astype(BF16))
        low = jnp.where(ii > jj, qk[:, :c] * dec, 0.0)
        attn = qk[:, c:] * dec
        tmat = _unit_lower_inverse(low, c, _bdot3)
        rhs = jnp.stack([jnp.concatenate([vb[:, hs], kbe[:, hs]], axis=1) for hs in heads]).astype(BF16)
        uw = _bdot(tmat.astype(BF16), rhs)
        for h, hs in enumerate(heads):
            u_ref[rows, hs] = uw[h, :, :HEAD_DIM]
            w_ref[rows, hs] = uw[h, :, HEAD_DIM:].astype(BF16)
            at_ref[rows, h * c:(h + 1) * c] = attn[h].astype(BF16)


def delta_wy(q, k, v, bb, gb, nch):
    m = q.shape[0]
    rows = nch * CHUNK
    row = pl.BlockSpec((rows, C_WIDTH), lambda i: (i, 0))
    sds = lambda w, dt: jax.ShapeDtypeStruct((m, w), dt)
    return pl.pallas_call(
        functools.partial(_delta_wy_kernel, nch=nch),
        out_shape=[sds(C_WIDTH, F32), sds(C_WIDTH, BF16), sds(C_HEADS * CHUNK, BF16), sds(C_WIDTH, BF16),
                   sds(C_WIDTH, BF16), jax.ShapeDtypeStruct((m // CHUNK, 8, C_WIDTH), F32)],
        grid=(m // rows,),
        in_specs=[row] * 5,
        out_specs=[row, row, pl.BlockSpec((rows, C_HEADS * CHUNK), lambda i: (i, 0)), row, row,
                   pl.BlockSpec((nch, 8, C_WIDTH), lambda i: (i, 0, 0))],
        compiler_params=_cp("parallel"),
        name="delta_wy",
    )(q, k, v, bb, gb)


def _delta_rec_kernel(u_ref, w_ref, at_ref, qg_ref, kd_ref, egl_ref, z_ref, s0_ref, nw_ref, o_ref, sout_ref, s_ref,
                      *, n_seq):
    ci = pl.program_id(0)
    c = CHUNK

    @pl.when(ci == 0)
    def _():
        s_ref[...] = s0_ref[...]

    nw = nw_ref[...]
    for n in range(n_seq):
        idx = [(n * C_HEADS + h, slice(h * HEAD_DIM, (h + 1) * HEAD_DIM)) for h in range(C_HEADS)]
        st = [s_ref[i] for i, _ in idx]
        r = [_dot(jnp.concatenate([w_ref[n, :, hs], qg_ref[n, :, hs]], axis=0), s.astype(BF16))
             for (_, hs), s in zip(idx, st)]
        vn = [(u_ref[n, :, hs] - rr[:c]).astype(BF16) for (_, hs), rr in zip(idx, r)]
        for h, ((i, hs), s, rr, vb) in enumerate(zip(idx, st, r, vn)):
            o = rr[c:] + _dot(at_ref[n, :, h * c:(h + 1) * c], vb)
            s_ref[i] = s * egl_ref[n, 0, 0:1, hs] + _dot_tn(kd_ref[n, :, hs], vb)
            on = o * lax.rsqrt(jnp.mean(o * o, axis=-1, keepdims=True) + EPS) * nw * _silu(z_ref[n, :, hs])
            o_ref[n, :, hs] = on.astype(o_ref.dtype)

    @pl.when(ci == pl.num_programs(0) - 1)
    def _():
        sout_ref[...] = s_ref[...]


def delta_rec(u, w, at, qg, kd, egl, z_arr, z_colblk, s0, norm_w, n_seq, seq):
    nc = seq // CHUNK
    v3 = lambda x: x.reshape(n_seq, seq, x.shape[-1])
    row = pl.BlockSpec((n_seq, CHUNK, C_WIDTH), lambda t: (0, t, 0))
    s_spec = pl.BlockSpec((n_seq * C_HEADS, HEAD_DIM, HEAD_DIM), lambda t: (0, 0, 0))
    o, s_fin = pl.pallas_call(
        functools.partial(_delta_rec_kernel, n_seq=n_seq),
        out_shape=[jax.ShapeDtypeStruct((n_seq, seq, C_WIDTH), BF16),
                   jax.ShapeDtypeStruct((n_seq * C_HEADS, HEAD_DIM, HEAD_DIM), F32)],
        grid=(nc,),
        in_specs=[row, row, pl.BlockSpec((n_seq, CHUNK, C_HEADS * CHUNK), lambda t: (0, t, 0)), row, row,
                  pl.BlockSpec((n_seq, 1, 8, C_WIDTH), lambda t: (0, t, 0, 0)),
                  pl.BlockSpec((n_seq, CHUNK, C_WIDTH), lambda t: (0, t, z_colblk)),
                  s_spec, pl.BlockSpec((1, HEAD_DIM), lambda t: (0, 0))],
        out_specs=[row, s_spec],
        scratch_shapes=[pltpu.VMEM((n_seq * C_HEADS, HEAD_DIM, HEAD_DIM), F32)],
        compiler_params=_cp("arbitrary"),
        name="delta_rec",
    )(v3(u), v3(w), v3(at), v3(qg), v3(kd), egl.reshape(n_seq, nc, 8, C_WIDTH), v3(z_arr),
      s0.reshape(n_seq * C_HEADS, HEAD_DIM, HEAD_DIM), norm_w.reshape(1, HEAD_DIM))
    return o.reshape(n_seq * seq, C_WIDTH), s_fin.reshape(n_seq, C_HEADS, HEAD_DIM, HEAD_DIM)


def _delta_chunk_kernel(q_ref, k_ref, v_ref, bb_ref, gb_ref, z_ref, s0_ref, nw_ref, o_ref, sout_ref, s_ref, *, c):
    ci = pl.program_id(1)

    @pl.when(ci == 0)
    def _():
        s_ref[...] = s0_ref[0]

    ii = lax.broadcasted_iota(jnp.int32, (c, c), 0)
    jj = lax.broadcasted_iota(jnp.int32, (c, c), 1)
    tri = (ii >= jj).astype(F32)
    sel0 = (lax.broadcasted_iota(jnp.int32, (c, HEAD_DIM), 1) == 0).astype(F32)
    nw = nw_ref[...]
    heads = [slice(h * HEAD_DIM, (h + 1) * HEAD_DIM) for h in range(C_HEADS)]
    q = [q_ref[:, hs] for hs in heads]
    k = [k_ref[:, hs] for hs in heads]
    bb = [bb_ref[:, hs] for hs in heads]
    gcb = [_dot(tri, gb_ref[:, hs], HI) for hs in heads]
    gcr = [_dot_nt(sel0, g, HI) for g in gcb]
    dec = [jnp.where(ii >= jj, jnp.exp(jnp.minimum(g[:, :c] - r, 0.0)), 0.0) for g, r in zip(gcb, gcr)]
    kb = [kh * bh for kh, bh in zip(k, bb)]
    low = jnp.stack([jnp.where(ii > jj, _dot_nt(kbh, kh, HI) * d, 0.0) for kbh, kh, d in zip(kb, k, dec)])
    tmat = _unit_lower_inverse(low, c, _bdot_hi)
    eg = [jnp.exp(g) for g in gcb]
    u = [_dot(tmat[h], v_ref[:, hs] * bb[h], HI) for h, hs in enumerate(heads)]
    w = [_dot(tmat[h], kb[h] * eg[h], HI) for h in range(C_HEADS)]
    s = [s_ref[h] for h in range(C_HEADS)]
    v_new = [uh - _dot(wh, sh, HI) for uh, wh, sh in zip(u, w, s)]
    attn = [_dot_nt(qh, kh, HI) * d for qh, kh, d in zip(q, k, dec)]
    o = [_dot(qh * e, sh, HI) + _dot(a, vn, HI) for qh, e, sh, a, vn in zip(q, eg, s, attn, v_new)]
    for h, hs in enumerate(heads):
        g_last = gcb[h][c - 1:c, :]
        s_ref[h] = s[h] * jnp.exp(g_last) + _dot_tn(k[h] * jnp.exp(g_last - gcb[h]), v_new[h], HI)
        on = o[h] * lax.rsqrt(jnp.mean(o[h] * o[h], axis=-1, keepdims=True) + EPS) * nw * _silu(z_ref[:, hs])
        o_ref[:, hs] = on.astype(o_ref.dtype)

    @pl.when(ci == pl.num_programs(1) - 1)
    def _():
        sout_ref[0] = s_ref[...]


def delta_chunks(q, k, v, bb, gb, z_arr, z_colblk, s0, s0_off, norm_w, n_seq, c, out_dtype):
    m = q.shape[0]
    nc = m // n_seq // c
    row = pl.BlockSpec((c, C_WIDTH), lambda n, t: (n * nc + t, 0))
    s_spec = pl.BlockSpec((1, C_HEADS, HEAD_DIM, HEAD_DIM), lambda n, t: (n, 0, 0, 0))
    s0_spec = pl.BlockSpec((1, C_HEADS, HEAD_DIM, HEAD_DIM), lambda n, t: (s0_off + n, 0, 0, 0))
    return pl.pallas_call(
        functools.partial(_delta_chunk_kernel, c=c),
        out_shape=[jax.ShapeDtypeStruct((m, C_WIDTH), out_dtype),
                   jax.ShapeDtypeStruct((n_seq, C_HEADS, HEAD_DIM, HEAD_DIM), F32)],
        grid=(n_seq, nc),
        in_specs=[row, row, row, row, row,
                  pl.BlockSpec((c, C_WIDTH), lambda n, t: (n * nc + t, z_colblk)),
                  s0_spec, pl.BlockSpec((1, HEAD_DIM), lambda n, t: (0, 0))],
        out_specs=[row, s_spec],
        scratch_shapes=[pltpu.VMEM((C_HEADS, HEAD_DIM, HEAD_DIM), F32)],
        compiler_params=_cp("parallel", "arbitrary"),
        name="delta_chunks",
    )(q, k, v, bb, gb, z_arr, s0, norm_w.reshape(1, HEAD_DIM))


def _merge_kernel(oa_ref, ob_ref, oc_ref, ga_ref, gb_ref, gc_ref, wa_ref, wb_ref, wc_ref, o_ref, *, prec):
    ya = _dot(oa_ref[...], wa_ref[...], prec)
    yb = _dot(ob_ref[...], wb_ref[...], prec)
    yc = _dot(oc_ref[...], wc_ref[...], prec)
    t = _sigmoid(ga_ref[...]) * ya + _sigmoid(gb_ref[...]) * yb + _sigmoid(gc_ref[...]) * yc
    o_ref[...] = t.astype(o_ref.dtype)


def merge_branches(oa, ob, oc, p, w_a, w_b, w_c, tm, tn):
    m = oa.shape[0]
    gate0 = 10240 // tn
    per = D_MODEL // tn
    o_spec = pl.BlockSpec((tm, 1024), lambda i, j: (i, 0))
    w_spec = pl.BlockSpec((1024, tn), lambda i, j: (0, j))
    gate_spec = lambda b: pl.BlockSpec((tm, tn), lambda i, j: (i, gate0 + b * per + j))
    return pl.pallas_call(
        functools.partial(_merge_kernel, prec=_prec(w_a)),
        out_shape=jax.ShapeDtypeStruct((m, D_MODEL), w_a.dtype),
        grid=(m // tm, D_MODEL // tn),
        in_specs=[o_spec, o_spec, o_spec, gate_spec(0), gate_spec(1), gate_spec(2), w_spec, w_spec, w_spec],
        out_specs=pl.BlockSpec((tm, tn), lambda i, j: (i, j)),
        compiler_params=_cp("parallel", "arbitrary"),
        name="merge_branches",
    )(oa, ob, oc, p, p, p, w_a, w_b, w_c)


def _router_kernel(x_ref, g_ref, r_ref, h_ref, sel_ref):
    x = x_ref[...]
    hn = x * lax.rsqrt(jnp.mean(x * x, axis=-1, keepdims=True) + EPS) * g_ref[...]
    h_ref[...] = hn
    logits = _dot(hn, r_ref[...], HI)
    lane_i = lax.broadcasted_iota(jnp.int32, logits.shape, 1)
    lane = lane_i.astype(F32)
    logits = jnp.where(lane_i < N_EXPERTS, logits, NEG)
    v1 = jnp.max(logits, axis=-1, keepdims=True)
    i1 = jnp.min(jnp.where(logits == v1, lane, 128.0), axis=-1, keepdims=True)
    rest = jnp.where(lane == i1, NEG, logits)
    v2 = jnp.max(rest, axis=-1, keepdims=True)
    i2 = jnp.min(jnp.where(rest == v2, lane, 128.0), axis=-1, keepdims=True)
    e = jnp.exp(v2 - v1)
    tot = 1.0 + e
    sel_ref[...] = jnp.where(lane_i == 0, i1, jnp.where(lane_i == 1, i2, jnp.where(lane_i == 2, 1.0 / tot, e / tot)))


def router(x, g, router_w, tm):
    m, d = x.shape
    rw = jnp.zeros((d, 128), F32).at[:, :N_EXPERTS].set(router_w)
    return pl.pallas_call(
        _router_kernel,
        out_shape=[jax.ShapeDtypeStruct((m, d), F32), jax.ShapeDtypeStruct((m, 128), F32)],
        grid=(m // tm,),
        in_specs=[pl.BlockSpec((tm, d), lambda i: (i, 0)), pl.BlockSpec((1, d), lambda i: (0, 0)),
                  pl.BlockSpec((d, 128), lambda i: (0, 0))],
        out_specs=[pl.BlockSpec((tm, d), lambda i: (i, 0)), pl.BlockSpec((tm, 128), lambda i: (i, 0))],
        compiler_params=_cp("parallel"),
        name="router",
    )(x, g.reshape(1, d), rw)


MOE_TM = 256


def _moe_plan(sel, m):
    i1 = sel[:, 0].astype(jnp.int32)
    i2 = sel[:, 1].astype(jnp.int32)
    e_flat = jnp.concatenate([i1, i2])
    tok = jnp.tile(jnp.arange(m, dtype=jnp.int32), 2)
    onehot = (e_flat[:, None] == jnp.arange(N_EXPERTS, dtype=jnp.int32)[None, :]).astype(jnp.int32)
    counts = jnp.sum(onehot, axis=0)
    rank = jnp.take_along_axis(jnp.cumsum(onehot, axis=0) - onehot, e_flat[:, None], axis=1)[:, 0]
    padded = (counts + MOE_TM - 1) // MOE_TM * MOE_TM
    ends = jnp.cumsum(padded)
    pos = (ends - padded)[e_flat] + rank
    n_tiles = (2 * m) // MOE_TM + N_EXPERTS
    row_token = jnp.zeros((n_tiles * MOE_TM,), jnp.int32).at[pos].set(tok)
    tiles = jnp.arange(n_tiles, dtype=jnp.int32)
    n_used = ends[-1] // MOE_TM
    tile_src = jnp.minimum(tiles, n_used - 1)
    tile_expert = jnp.sum((tile_src * MOE_TM)[:, None] >= ends[None, :], axis=1).astype(jnp.int32)
    tile_used = (tiles < n_used).astype(jnp.int32)
    return row_token, pos[:m], pos[m:], tile_expert, tile_src, tile_used


def _moe_gather_kernel(tok_ref, used_ref, h_hbm, o_ref, buf, sem):
    i = pl.program_id(0)

    @pl.when(used_ref[i] == 1)
    def _():
        def issue(r, carry):
            pltpu.make_async_copy(h_hbm.at[pl.ds(tok_ref[i * MOE_TM + r], 1)], buf.at[pl.ds(r, 1)], sem).start()
            return carry
        lax.fori_loop(0, MOE_TM, issue, 0)
        pltpu.make_async_copy(h_hbm.at[pl.ds(0, MOE_TM)], buf, sem).wait()
        o_ref[...] = buf[...].astype(o_ref.dtype)

    @pl.when(used_ref[i] == 0)
    def _():
        o_ref[...] = jnp.zeros_like(o_ref)


def moe_gather(h, row_token, tile_used):
    d = h.shape[1]
    n_tiles = tile_used.shape[0]
    return pl.pallas_call(
        _moe_gather_kernel,
        out_shape=jax.ShapeDtypeStruct((n_tiles * MOE_TM, d), BF16),
        grid_spec=pltpu.PrefetchScalarGridSpec(
            num_scalar_prefetch=2,
            grid=(n_tiles,),
            in_specs=[pl.BlockSpec(memory_space=pl.ANY)],
            out_specs=pl.BlockSpec((MOE_TM, d), lambda i, tok, used: (i, 0)),
            scratch_shapes=[pltpu.VMEM((MOE_TM, d), F32), pltpu.SemaphoreType.DMA(())]),
        compiler_params=_cp("arbitrary"),
        name="moe_gather",
    )(row_token, tile_used, h)


def _moe_up_kernel(te_ref, ts_ref, tu_ref, h_ref, wg_ref, wu_ref, o_ref, wg_bf, wu_bf):
    t = pl.program_id(1)
    used = tu_ref[t] == 1
    new_block = jnp.logical_or(t == 0, te_ref[t] != te_ref[jnp.maximum(t - 1, 0)])

    @pl.when(jnp.logical_and(used, new_block))
    def _():
        wg_bf[...] = wg_ref[0].astype(BF16)
        wu_bf[...] = wu_ref[0].astype(BF16)

    @pl.when(used)
    def _():
        h = h_ref[...]
        o_ref[...] = (_silu(_dot(h, wg_bf[...])) * _dot(h, wu_bf[...])).astype(o_ref.dtype)

    @pl.when(jnp.logical_not(used))
    def _():
        o_ref[...] = jnp.zeros_like(o_ref)


def moe_up(hs, wg, wu, plan, tn):
    tile_expert, tile_src, tile_used = plan
    d, f = wg.shape[1], wg.shape[2]
    n_tiles = tile_used.shape[0]
    w_spec = pl.BlockSpec((1, d, tn), lambda j, t, te, ts, tu: (te[t], 0, j))
    return pl.pallas_call(
        _moe_up_kernel,
        out_shape=jax.ShapeDtypeStruct((n_tiles * MOE_TM, f), BF16),
        grid_spec=pltpu.PrefetchScalarGridSpec(
            num_scalar_prefetch=3,
            grid=(f // tn, n_tiles),
            in_specs=[pl.BlockSpec((MOE_TM, d), lambda j, t, te, ts, tu: (ts[t], 0)), w_spec, w_spec],
            out_specs=pl.BlockSpec((MOE_TM, tn), lambda j, t, te, ts, tu: (t, j)),
            scratch_shapes=[pltpu.VMEM((d, tn), BF16), pltpu.VMEM((d, tn), BF16)]),
        compiler_params=_cp("arbitrary", "arbitrary"),
        name="moe_up",
    )(tile_expert, tile_src, tile_used, hs, wg, wu)


def _moe_down_kernel(te_ref, ts_ref, tu_ref, a_ref, w_ref, o_ref, w_bf):
    t = pl.program_id(1)
    used = tu_ref[t] == 1
    new_block = jnp.logical_or(t == 0, te_ref[t] != te_ref[jnp.maximum(t - 1, 0)])

    @pl.when(jnp.logical_and(used, new_block))
    def _():
        w_bf[...] = w_ref[0].astype(BF16)

    @pl.when(used)
    def _():
        o_ref[...] = _dot(a_ref[...], w_bf[...])

    @pl.when(jnp.logical_not(used))
    def _():
        o_ref[...] = jnp.zeros_like(o_ref)


def moe_down(act, wd, plan, tn):
    tile_expert, tile_src, tile_used = plan
    f, d = wd.shape[1], wd.shape[2]
    n_tiles = tile_used.shape[0]
    return pl.pallas_call(
        _moe_down_kernel,
        out_shape=jax.ShapeDtypeStruct((n_tiles * MOE_TM, d), F32),
        grid_spec=pltpu.PrefetchScalarGridSpec(
            num_scalar_prefetch=3,
            grid=(d // tn, n_tiles),
            in_specs=[pl.BlockSpec((MOE_TM, f), lambda j, t, te, ts, tu: (ts[t], 0)),
                      pl.BlockSpec((1, f, tn), lambda j, t, te, ts, tu: (te[t], 0, j))],
            out_specs=pl.BlockSpec((MOE_TM, tn), lambda j, t, te, ts, tu: (t, j)),
            scratch_shapes=[pltpu.VMEM((f, tn), BF16)]),
        compiler_params=_cp("arbitrary", "arbitrary"),
        name="moe_down",
    )(tile_expert, tile_src, tile_used, act, wd)


def _moe_combine_kernel(p1_ref, p2_ref, x_ref, sel_ref, y_hbm, o_ref, b1, b2, sems, *, tm):
    base = pl.program_id(0) * tm

    def issue(r, carry):
        pltpu.make_async_copy(y_hbm.at[pl.ds(p1_ref[base + r], 1)], b1.at[pl.ds(r, 1)], sems.at[0]).start()
        pltpu.make_async_copy(y_hbm.at[pl.ds(p2_ref[base + r], 1)], b2.at[pl.ds(r, 1)], sems.at[1]).start()
        return carry
    lax.fori_loop(0, tm, issue, 0)
    pltpu.make_async_copy(y_hbm.at[pl.ds(0, tm)], b1, sems.at[0]).wait()
    pltpu.make_async_copy(y_hbm.at[pl.ds(0, tm)], b2, sems.at[1]).wait()
    o_ref[...] = x_ref[...] + (sel_ref[:, 2:3] * b1[...] + sel_ref[:, 3:4] * b2[...])


def moe_combine(x, sel, y, pos1, pos2, tm):
    m, d = x.shape
    return pl.pallas_call(
        functools.partial(_moe_combine_kernel, tm=tm),
        out_shape=jax.ShapeDtypeStruct((m, d), F32),
        grid_spec=pltpu.PrefetchScalarGridSpec(
            num_scalar_prefetch=2,
            grid=(m // tm,),
            in_specs=[pl.BlockSpec((tm, d), lambda i, p1, p2: (i, 0)),
                      pl.BlockSpec((tm, 128), lambda i, p1, p2: (i, 0)),
                      pl.BlockSpec(memory_space=pl.ANY)],
            out_specs=pl.BlockSpec((tm, d), lambda i, p1, p2: (i, 0)),
            scratch_shapes=[pltpu.VMEM((tm, d), F32), pltpu.VMEM((tm, d), F32), pltpu.SemaphoreType.DMA((2,))]),
        compiler_params=_cp("arbitrary"),
        name="moe_combine",
    )(pos1, pos2, x, sel, y)


def moe_ffn(x, g, router_w, wg, wu, wd):
    m = x.shape[0]
    h, sel = router(x, g, router_w, 320)
    row_token, pos1, pos2, tile_expert, tile_src, tile_used = _moe_plan(sel, m)
    plan = (tile_expert, tile_src, tile_used)
    hs = moe_gather(h, row_token, tile_used)
    act = moe_up(hs, wg, wu, plan, 1024)
    y = moe_down(act, wd, plan, 512)
    return moe_combine(x, sel, y, pos1, pos2, 128)


def _split_w_in(w):
    qa, ka, va, qb, kb, vb, cqkv, z, cb, ca, gates = jnp.split(w, np.cumsum(PROJ_SIZES)[:-1].tolist(), axis=-1)
    main = jnp.concatenate([qa, qb, ka, va, kb, vb, cqkv, z, gates], axis=-1)
    small = jnp.concatenate([cb, ca, jnp.zeros((w.shape[0], 112), w.dtype)], axis=-1)
    return main, small


def _mix_prompt(h, lp):
    w_main, w_small, sink, conv_w, a_log, dt_bias, norm_w, w_a, w_b, w_c = lp
    m = h.shape[0]
    p = matmul(h, w_main, 2048, 512)
    pc = matmul(h, w_small, 512, 128)
    p3 = p.reshape(BATCH, SEQ, PW)
    oa, = band_attention(p3, 1, 0, 16, 17, _alibi(A_HEADS), sink, False, BF16)
    slopes_b = _alibi(N_BGROUPS * B_HPG).reshape(N_BGROUPS, B_HPG)
    outs, lses = [], []
    for gi, (_, dil) in enumerate(B_GROUPS):
        o, lse = band_attention(p3, dil, 1 + gi, 18 + gi, 21 + gi, slopes_b[gi], None, True, F32)
        outs.append(o)
        lses.append(lse)
    ob = combine_groups(outs, lses, 512, BF16)
    tm = 256
    q, k, v, bb, gb = delta_prep(p, 2, p, 2, lambda i: jnp.maximum(i * (tm // 8) - 1, 0), pc, conv_w, a_log,
                                 dt_bias, tm, SEQ // tm, tm)
    s0 = jnp.zeros((BATCH, C_HEADS, HEAD_DIM, HEAD_DIM), F32)
    u, w, at, qg, kd, egl = delta_wy(q, k, v, bb, gb, 2)
    oc, s_fin = delta_rec(u, w, at, qg, kd, egl, p, 9, s0, norm_w, BATCH, SEQ)
    t = merge_branches(oa, ob, oc, p, w_a, w_b, w_c, 1024, 512)
    p4 = p.reshape(BATCH, SEQ, PW)
    kv = lambda blk, win: p4[:, SEQ - win:, blk * 256:(blk + 1) * 256].reshape(BATCH, win, 2, HEAD_DIM)
    state = [kv(16, A_WINDOW), kv(17, A_WINDOW)]
    for gi, (win, _) in enumerate(B_GROUPS):
        state += [kv(18 + gi, win), kv(21 + gi, win)]
    state += [p4[:, SEQ - (CONV_K - 1):, 6144:9216], s_fin]
    return t, state


def _mix_sample(h, lp, caches, layer):
    w_main, w_small, sink, conv_w, a_log, dt_bias, norm_w, w_a, w_b, w_c = lp
    c_ak, c_av, b1k, b1v, b2k, b2v, b3k, b3v, conv_all, s0_all = caches
    nb, nt = DEC_BATCH, DEC_SEQ
    m = nb * nt
    p = matmul(h, w_main, m, 512)
    pc = matmul(h, w_small, m, 128)
    p3 = p.reshape(nb, nt, PW)

    def heads_q(blk):
        qq = p3[:, :, blk * 1024:(blk + 1) * 1024].reshape(nb, nt, 2, 4, HEAD_DIM)
        return jnp.transpose(qq, (0, 2, 3, 1, 4)).reshape(nb, 2, 16, HEAD_DIM)

    def heads_o(o):
        oo = o.reshape(nb, 2, 4, nt, HEAD_DIM)
        return jnp.transpose(oo, (0, 3, 1, 2, 4)).reshape(m, 1024)

    def new_rows(blk):
        x = p3[:, :, blk * 256:(blk + 1) * 256]
        return x.reshape(nb, nt, 2, HEAD_DIM), jnp.pad(x, ((0, 0), (0, DEC_PAD - nt), (0, 0)))

    def reach(buf, dil):
        win, keep = buf.shape[2], min(dil, nt)
        x = buf.reshape(DEPTH * nb, win // dil, dil, 2, HEAD_DIM)[:, :, :keep]
        return x.reshape(DEPTH * nb, win // dil * keep, 256), win

    ka, ka_p = new_rows(16)
    va, va_p = new_rows(17)
    oa, _ = window_decode(heads_q(0), reach(c_ak, 1)[0], reach(c_av, 1)[0], layer, A_WINDOW, ka_p, va_p,
                          _alibi(A_HEADS), 1, sink)
    oa = heads_o(oa)
    appended = [ka, va]
    slopes_b = _alibi(N_BGROUPS * B_HPG).reshape(N_BGROUPS, B_HPG)
    outs, lses = [], []
    for gi, ((_, dil), (bk, bv)) in enumerate(zip(B_GROUPS, ((b1k, b1v), (b2k, b2v), (b3k, b3v)))):
        kn, kn_p = new_rows(18 + gi)
        vn, vn_p = new_rows(21 + gi)
        (rk, win), (rv, _) = reach(bk, dil), reach(bv, dil)
        o, lse = window_decode(heads_q(1 + gi), rk, rv, layer, win, kn_p, vn_p, slopes_b[gi], dil, None)
        outs.append(heads_o(o))
        lses.append(heads_o(lse))
        appended += [kn, vn]
    ob = combine_groups(outs, lses, m, F32)

    pad_rows = lambda x: jnp.pad(x.reshape(nb, nt, -1), ((0, 0), (0, DEC_PAD - nt), (0, 0))).reshape(nb * DEC_PAD, -1)
    cq = p3[:, :, 6144:9216]
    xs = pad_rows(cq)
    xprev = jnp.pad(conv_all[layer], ((0, 0), (8 - (CONV_K - 1), 0), (0, 0))).reshape(nb * 8, 3 * C_WIDTH)
    q, k, v, bb, gb = delta_prep(xs, 0, xprev, 0, lambda i: i, pad_rows(pc), conv_w, a_log, dt_bias,
                                 DEC_PAD, None, nt)
    zs = pad_rows(p3[:, :, 9216:10240])
    s0_flat = s0_all.reshape(DEPTH * nb, C_HEADS, HEAD_DIM, HEAD_DIM)
    oc, s_fin = delta_chunks(q, k, v, bb, gb, zs, 0, s0_flat, layer * nb, norm_w, nb, DEC_PAD, F32)
    oc = oc.reshape(nb, DEC_PAD, C_WIDTH)[:, :nt].reshape(m, C_WIDTH)
    t = merge_branches(oa, ob, oc, p, w_a, w_b, w_c, m, 512)
    appended += [cq, s_fin]
    return t, appended


def kernel(x_prompt, x_sample, cache_a_k, cache_a_v, cache_b1_k, cache_b1_v, cache_b2_k, cache_b2_v, cache_b3_k, cache_b3_v, state_c_conv, state_c_rec, norm_mix, norm_ffn, norm_final, w_in, attn_sink, conv_w, a_log, dt_bias, norm_delta, w_out_a, w_out_b, w_out_c, w_out, ffn_w_gate, ffn_w_up, ffn_w_down, router_w, moe_w_gate, moe_w_up, moe_w_down):
    mp, ms = BATCH * SEQ, DEC_BATCH * DEC_SEQ
    xp = x_prompt.reshape(mp, D_MODEL)
    xs = x_sample.reshape(ms, D_MODEL)
    tmp = 1024
    p_states, s_states = [], []
    caches = (cache_a_k, cache_a_v, cache_b1_k, cache_b1_v, cache_b2_k, cache_b2_v, cache_b3_k, cache_b3_v,
              state_c_conv, state_c_rec)
    for l in range(DEPTH):
        w_main, w_small = _split_w_in(w_in[l])
        shared = (attn_sink[l], conv_w[l], a_log[l], dt_bias[l], norm_delta[l])
        lp_s = (w_main, w_small, *shared, w_out_a[l], w_out_b[l], w_out_c[l])
        lp_p = (w_main.astype(BF16), w_small.astype(BF16), *shared,
                w_out_a[l].astype(BF16), w_out_b[l].astype(BF16), w_out_c[l].astype(BF16))
        tp, sp = _mix_prompt(rmsnorm(xp, norm_mix[l], BF16, tmp), lp_p)
        ts, ss = _mix_sample(rmsnorm(xs, norm_mix[l], F32, ms), lp_s, caches, l)
        xp = matmul_residual(xp, tp, w_out[l].astype(BF16), tmp, 512)
        xs = matmul_residual(xs, ts, w_out[l], ms, 512)
        p_states.append(sp)
        s_states.append(ss)
        i = l // 2
        if l % 2 == 0:
            wg, wu, wd = ffn_w_gate[i], ffn_w_up[i], ffn_w_down[i]
            hp = rmsnorm(xp, norm_ffn[l], BF16, tmp)
            hs = rmsnorm(xs, norm_ffn[l], F32, ms)
            xp = matmul_residual(xp, swiglu_up(hp, wg.astype(BF16), wu.astype(BF16), tmp, 512), wd.astype(BF16),
                                 tmp, 512)
            xs = matmul_residual(xs, swiglu_up(hs, wg, wu, ms, 512), wd, ms, 512)
        else:
            x_all = moe_ffn(jnp.concatenate([xp, xs], axis=0), norm_ffn[l], router_w[i], moe_w_gate[i],
                            moe_w_up[i], moe_w_down[i])
            xp, xs = x_all[:mp], x_all[mp:]
    y_prompt = rmsnorm(xp, norm_final, F32, tmp).reshape(BATCH, SEQ, D_MODEL)
    y_sample = rmsnorm(xs, norm_final, F32, ms).reshape(DEC_BATCH, DEC_SEQ, D_MODEL)
    p_out = [jnp.stack(z) for z in zip(*p_states)]
    s_new = [jnp.stack(z) for z in zip(*s_states)]
    s_out = [jnp.concatenate([old, new], axis=2)[:, :, DEC_SEQ:] for old, new in zip(caches[:-1], s_new[:-1])]
    s_out.append(s_new[-1])
    return (y_prompt, y_sample, *p_out, *s_out)
```

```python
import functools

import numpy as np
import jax
import jax.numpy as jnp
from jax import lax
from jax.experimental import pallas as pl
from jax.experimental.pallas import tpu as pltpu

F32 = jnp.float32
BF16 = jnp.bfloat16
HI = lax.Precision.HIGHEST

D_MODEL = 2048
BATCH = 4
SEQ = 2048
DEPTH = 2
DEC_BATCH = 32
DEC_SEQ = 4
HEAD_DIM = 128
A_HEADS = 8
A_KV = 2
A_WINDOW = 128
B_GROUPS = ((128, 1), (512, 4), (2048, 16))
N_BGROUPS = 3
B_HPG = 8
B_KV = 2
C_HEADS = 8
C_WIDTH = C_HEADS * HEAD_DIM
CONV_K = 4
CHUNK = 64
BAND = 128
D_FF = 5632
N_EXPERTS = 8
D_FF_EXPERT = 7168
PROJ_SIZES = (1024, 256, 256, 3072, 768, 768, 3072, 1024, 8, 8, 6144)
EPS = 1e-6
NEG = -1e30

PW = 16384
DEC_PAD = 8
VMEM_LIMIT_BYTES = 56 * 1024 * 1024


def _cp(*sem):
    return pltpu.CompilerParams(dimension_semantics=sem, vmem_limit_bytes=VMEM_LIMIT_BYTES)


def _alibi(n):
    return np.asarray(2.0 ** (-8.0 * np.arange(1, n + 1) / n), dtype=np.float32)


def _sigmoid(x):
    return 1.0 / (1.0 + jnp.exp(-x))


def _silu(x):
    return x * _sigmoid(x)


def _softplus(x):
    return jnp.maximum(x, 0.0) + jnp.log(1.0 + jnp.exp(-jnp.abs(x)))


def _dot(a, b, prec=None):
    return jnp.dot(a, b, precision=prec, preferred_element_type=F32)


def _dot_nt(a, b, prec=None):
    return lax.dot_general(a, b, (((1,), (1,)), ((), ())), precision=prec, preferred_element_type=F32)


def _dot_tn(a, b, prec=None):
    return lax.dot_general(a, b, (((0,), (0,)), ((), ())), precision=prec, preferred_element_type=F32)


def _rmsnorm_kernel(x_ref, g_ref, o_ref):
    x = x_ref[...]
    ms = jnp.mean(x * x, axis=-1, keepdims=True)
    o_ref[...] = (x * lax.rsqrt(ms + EPS) * g_ref[...]).astype(o_ref.dtype)


def rmsnorm(x, g, out_dtype, tm):
    m, d = x.shape
    return pl.pallas_call(
        _rmsnorm_kernel,
        out_shape=jax.ShapeDtypeStruct((m, d), out_dtype),
        grid=(m // tm,),
        in_specs=[pl.BlockSpec((tm, d), lambda i: (i, 0)), pl.BlockSpec((1, d), lambda i: (0, 0))],
        out_specs=pl.BlockSpec((tm, d), lambda i: (i, 0)),
        compiler_params=_cp("parallel"),
        name="rmsnorm",
    )(x, g.reshape(1, d))


def _prec(x):
    return HI if x.dtype == F32 else None


def _mm_kernel(a_ref, w_ref, o_ref, *, prec):
    o_ref[...] = _dot(a_ref[...], w_ref[...], prec).astype(o_ref.dtype)


def matmul(a, w, tm, tn, out_dtype=F32):
    m, k = a.shape
    n = w.shape[1]
    return pl.pallas_call(
        functools.partial(_mm_kernel, prec=_prec(w)),
        out_shape=jax.ShapeDtypeStruct((m, n), out_dtype),
        grid=(m // tm, n // tn),
        in_specs=[pl.BlockSpec((tm, k), lambda i, j: (i, 0)), pl.BlockSpec((k, tn), lambda i, j: (0, j))],
        out_specs=pl.BlockSpec((tm, tn), lambda i, j: (i, j)),
        compiler_params=_cp("parallel", "arbitrary"),
        name="matmul",
    )(a, w)


def _mm_res_kernel(x_ref, a_ref, w_ref, o_ref, *, prec):
    o_ref[...] = x_ref[...] + _dot(a_ref[...], w_ref[...], prec)


def matmul_residual(x, a, w, tm, tn):
    m, k = a.shape
    n = w.shape[1]
    return pl.pallas_call(
        functools.partial(_mm_res_kernel, prec=_prec(w)),
        out_shape=jax.ShapeDtypeStruct((m, n), F32),
        grid=(m // tm, n // tn),
        in_specs=[pl.BlockSpec((tm, tn), lambda i, j: (i, j)),
                  pl.BlockSpec((tm, k), lambda i, j: (i, 0)),
                  pl.BlockSpec((k, tn), lambda i, j: (0, j))],
        out_specs=pl.BlockSpec((tm, tn), lambda i, j: (i, j)),
        compiler_params=_cp("parallel", "arbitrary"),
        name="matmul_residual",
    )(x, a, w)


def _swiglu_kernel(h_ref, wg_ref, wu_ref, o_ref, *, prec):
    h = h_ref[...]
    a = _dot(h, wg_ref[...], prec)
    b = _dot(h, wu_ref[...], prec)
    o_ref[...] = (_silu(a) * b).astype(o_ref.dtype)


def swiglu_up(h, wg, wu, tm, tn):
    m, k = h.shape
    n = wg.shape[1]
    return pl.pallas_call(
        functools.partial(_swiglu_kernel, prec=_prec(wg)),
        out_shape=jax.ShapeDtypeStruct((m, n), wg.dtype),
        grid=(m // tm, n // tn),
        in_specs=[pl.BlockSpec((tm, k), lambda i, j: (i, 0)),
                  pl.BlockSpec((k, tn), lambda i, j: (0, j)),
                  pl.BlockSpec((k, tn), lambda i, j: (0, j))],
        out_specs=pl.BlockSpec((tm, tn), lambda i, j: (i, j)),
        compiler_params=_cp("parallel", "arbitrary"),
        name="swiglu_up",
    )(h, wg, wu)


def _band_scan(q_ref, k_ref, v_ref, dil, slope, sink, emit):
    seq = q_ref.shape[1]
    span = BAND * dil
    qi = lax.broadcasted_iota(jnp.int32, (BAND, 2 * BAND), 0)
    ki = lax.broadcasted_iota(jnp.int32, (BAND, 2 * BAND), 1)
    dist = qi + BAND - ki
    band = (dist >= 0) & (dist <= BAND)
    distf = dist.astype(F32) * float(dil)
    scale = HEAD_DIM ** -0.5

    def block(it, carry):
        r = it % dil
        b = it // dil
        cur = pl.ds(b * span + r, BAND, stride=dil)
        prv = pl.ds(jnp.maximum(b - 1, 0) * span + r, BAND, stride=dil)
        k = jnp.concatenate([k_ref[0, prv, :], k_ref[0, cur, :]], axis=0).astype(BF16)
        v = jnp.concatenate([v_ref[0, prv, :], v_ref[0, cur, :]], axis=0).astype(BF16)
        mask = band & ((ki >= BAND) | (b > 0))
        q = (q_ref[0, cur, :] * scale).astype(BF16)
        s = jnp.where(mask, _dot_nt(q, k) - slope * distf, NEG)
        m = jnp.max(s, axis=-1, keepdims=True)
        if sink is not None:
            m = jnp.maximum(m, sink)
        e = jnp.exp(s - m)
        den = jnp.sum(e, axis=-1, keepdims=True)
        if sink is not None:
            den = den + jnp.exp(sink - m)
        emit(cur, _dot(e.astype(BF16), v) / den, m + jnp.log(den))
        return carry

    lax.fori_loop(0, seq // BAND, block, 0, unroll=8)


def _window_kernel(slope_ref, sink_ref, q_ref, k_ref, v_ref, o_ref):
    h = pl.program_id(1)

    def emit(rows, o, lse):
        o_ref[0, rows, :] = o.astype(o_ref.dtype)

    _band_scan(q_ref, k_ref, v_ref, 1, slope_ref[h], sink_ref[h], emit)


def _dilated_kernel(slope_ref, *refs):
    q_refs, k_refs, v_refs = refs[0:3], refs[3:6], refs[6:9]
    o_ref, o_scr, l_scr = refs[9:]
    h = pl.program_id(1)
    seq = o_ref.shape[1]
    for gi, (_, dil) in enumerate(B_GROUPS):
        def emit(rows, o, lse, gi=gi):
            o_scr[gi, rows, :] = o
            l_scr[gi, rows, :] = jnp.broadcast_to(lse, (BAND, HEAD_DIM))

        _band_scan(q_refs[gi], k_refs[gi], v_refs[gi], dil, slope_ref[gi * B_HPG + h], None, emit)

    def merge(c, carry):
        rows = pl.ds(pl.multiple_of(c * 256, 256), 256)
        la, lb, lc = l_scr[0, rows, :], l_scr[1, rows, :], l_scr[2, rows, :]
        m = jnp.maximum(jnp.maximum(la, lb), lc)
        wa, wb, wc = jnp.exp(la - m), jnp.exp(lb - m), jnp.exp(lc - m)
        out = (wa * o_scr[0, rows, :] + wb * o_scr[1, rows, :] + wc * o_scr[2, rows, :]) / (wa + wb + wc)
        o_ref[0, rows, :] = out.astype(o_ref.dtype)
        return carry

    lax.fori_loop(0, seq // 256, merge, 0)


def window_attention(p3, slopes, sink):
    n_seq, seq, _ = p3.shape
    smem = pl.BlockSpec(memory_space=pltpu.SMEM)
    o = pl.pallas_call(
        _window_kernel,
        out_shape=jax.ShapeDtypeStruct((n_seq, seq, 1024), BF16),
        grid=(n_seq, A_HEADS),
        in_specs=[smem, smem, pl.BlockSpec((1, seq, HEAD_DIM), lambda n, h: (n, 0, h)),
                  pl.BlockSpec((1, seq, HEAD_DIM), lambda n, h: (n, 0, 32 + h // 4)),
                  pl.BlockSpec((1, seq, HEAD_DIM), lambda n, h: (n, 0, 34 + h // 4))],
        out_specs=pl.BlockSpec((1, seq, HEAD_DIM), lambda n, h: (n, 0, h)),
        compiler_params=_cp("parallel", "arbitrary"),
        name="window_attention",
    )(jnp.asarray(slopes, F32), sink, p3, p3, p3)
    return o.reshape(n_seq * seq, 1024)


def dilated_attention(p3, slopes):
    n_seq, seq, _ = p3.shape
    col = lambda f: pl.BlockSpec((1, seq, HEAD_DIM), f)
    q_specs = [col(lambda n, h, gi=gi: (n, 0, 8 * (1 + gi) + h)) for gi in range(N_BGROUPS)]
    k_specs = [col(lambda n, h, gi=gi: (n, 0, 2 * (18 + gi) + h // 4)) for gi in range(N_BGROUPS)]
    v_specs = [col(lambda n, h, gi=gi: (n, 0, 2 * (21 + gi) + h // 4)) for gi in range(N_BGROUPS)]
    o = pl.pallas_call(
        _dilated_kernel,
        out_shape=jax.ShapeDtypeStruct((n_seq, seq, 1024), BF16),
        grid=(n_seq, B_HPG),
        in_specs=[pl.BlockSpec(memory_space=pltpu.SMEM)] + q_specs + k_specs + v_specs,
        out_specs=pl.BlockSpec((1, seq, HEAD_DIM), lambda n, h: (n, 0, h)),
        scratch_shapes=[pltpu.VMEM((N_BGROUPS, seq, HEAD_DIM), F32), pltpu.VMEM((N_BGROUPS, seq, HEAD_DIM), F32)],
        compiler_params=_cp("parallel", "arbitrary"),
        name="dilated_attention",
    )(jnp.asarray(slopes, F32).reshape(-1), *([p3] * 9))
    return o.reshape(n_seq * seq, 1024)


def _decode_kernel(*refs, slopes, dil, win, has_sink):
    if has_sink:
        sink_ref, refs = refs[0], refs[1:]
    q_ref, bk_ref, bv_ref, nk_ref, nv_ref, o_ref, lse_ref = refs
    rows = 4 * DEC_SEQ
    keep = min(dil, DEC_SEQ)
    held = win // dil * keep
    ri = lax.broadcasted_iota(jnp.int32, (rows, held), 0)
    ci = lax.broadcasted_iota(jnp.int32, (rows, held), 1)
    wpos = (ci // keep) * dil + ci % keep
    t1 = win + (ri & 3) - wpos
    valid1 = (wpos >= (ri & 3)) & ((t1 & (dil - 1)) == 0)
    ri2 = lax.broadcasted_iota(jnp.int32, (rows, DEC_PAD), 0)
    ci2 = lax.broadcasted_iota(jnp.int32, (rows, DEC_PAD), 1)
    t2 = (ri2 & 3) - ci2
    valid2 = (t2 >= 0) & ((t2 & (dil - 1)) == 0) & (ci2 < DEC_SEQ)
    hr = lax.broadcasted_iota(jnp.int32, (rows, 1), 0) >> 2
    scale = HEAD_DIM ** -0.5
    for g in range(2):
        lo = g * HEAD_DIM
        slope = jnp.zeros((rows, 1), F32)
        for r in range(4):
            slope = jnp.where(hr == r, float(slopes[g * 4 + r]), slope)
        q = q_ref[0, g] * scale
        s1 = _dot_nt(q, bk_ref[0, :, lo:lo + HEAD_DIM], HI)
        s2 = _dot_nt(q, nk_ref[0, :, lo:lo + HEAD_DIM], HI)
        s1 = jnp.where(valid1, s1 - slope * t1.astype(F32), NEG)
        s2 = jnp.where(valid2, s2 - slope * t2.astype(F32), NEG)
        m = jnp.maximum(jnp.max(s1, axis=-1, keepdims=True), jnp.max(s2, axis=-1, keepdims=True))
        if has_sink:
            sink = jnp.zeros((rows, 1), F32)
            for r in range(4):
                sink = jnp.where(hr == r, sink_ref[g * 4 + r], sink)
            m = jnp.maximum(m, sink)
        e1 = jnp.exp(s1 - m)
        e2 = jnp.exp(s2 - m)
        den = jnp.sum(e1, axis=-1, keepdims=True) + jnp.sum(e2, axis=-1, keepdims=True)
        if has_sink:
            den = den + jnp.exp(sink - m)
        o = _dot(e1, bv_ref[0, :, lo:lo + HEAD_DIM], HI) + _dot(e2, nv_ref[0, :, lo:lo + HEAD_DIM], HI)
        o_ref[0, g] = o / den
        lse_ref[0, g] = jnp.broadcast_to(m + jnp.log(den), (rows, HEAD_DIM))


def window_decode(q, buf_k, buf_v, layer, win, new_k, new_v, slopes, dil, sink):
    nb, held = q.shape[0], buf_k.shape[1]
    has_sink = sink is not None
    in_specs = [
        pl.BlockSpec((1, 2, 16, HEAD_DIM), lambda b: (b, 0, 0, 0)),
        pl.BlockSpec((1, held, 256), lambda b: (layer * nb + b, 0, 0)),
        pl.BlockSpec((1, held, 256), lambda b: (layer * nb + b, 0, 0)),
        pl.BlockSpec((1, DEC_PAD, 256), lambda b: (b, 0, 0)),
        pl.BlockSpec((1, DEC_PAD, 256), lambda b: (b, 0, 0)),
    ]
    ins = [q, buf_k, buf_v, new_k, new_v]
    if has_sink:
        in_specs = [pl.BlockSpec(memory_space=pltpu.SMEM)] + in_specs
        ins = [sink] + ins
    o_shape = jax.ShapeDtypeStruct((nb, 2, 16, HEAD_DIM), F32)
    o_spec = pl.BlockSpec((1, 2, 16, HEAD_DIM), lambda b: (b, 0, 0, 0))
    return pl.pallas_call(
        functools.partial(_decode_kernel, slopes=tuple(float(s) for s in slopes), dil=dil, win=win,
                          has_sink=has_sink),
        out_shape=[o_shape, o_shape],
        grid=(nb,),
        in_specs=in_specs,
        out_specs=[o_spec, o_spec],
        compiler_params=_cp("parallel"),
        name="window_decode",
    )(*ins)


def _combine_kernel(o1, o2, o3, l1, l2, l3, out_ref):
    a, b, c = l1[...], l2[...], l3[...]
    m = jnp.maximum(jnp.maximum(a, b), c)
    wa, wb, wc = jnp.exp(a - m), jnp.exp(b - m), jnp.exp(c - m)
    tot = wa + wb + wc
    out_ref[...] = ((wa * o1[...] + wb * o2[...] + wc * o3[...]) / tot).astype(out_ref.dtype)


def combine_groups(outs, lses, tm, out_dtype):
    m, d = outs[0].shape
    spec = pl.BlockSpec((tm, d), lambda i: (i, 0))
    return pl.pallas_call(
        _combine_kernel,
        out_shape=jax.ShapeDtypeStruct((m, d), out_dtype),
        grid=(m // tm,),
        in_specs=[spec] * 6,
        out_specs=spec,
        compiler_params=_cp("parallel"),
        name="combine_groups",
    )(*outs, *lses)


def _delta_prep_kernel(x_ref, xp_ref, pc_ref, cw_ref, alog_ref, dtb_ref, eb_ref, eg_ref,
                       q_ref, k_ref, v_ref, bb_ref, gb_ref, *, tm, tiles_per_seq, n_valid):
    i = pl.program_id(0)
    x = x_ref[...]
    prev = xp_ref[...]
    if tiles_per_seq is not None:
        prev = jnp.where(i % tiles_per_seq == 0, 0.0, prev)
    row8 = lax.broadcasted_iota(jnp.int32, (8, 1), 0)
    conv = x * cw_ref[CONV_K - 1:CONV_K, :]
    for back in range(1, CONV_K):
        xr = pltpu.roll(x, back, 0)
        head = jnp.where(row8 < back, pltpu.roll(prev, back, 0), xr[:8])
        sh = head if tm == 8 else jnp.concatenate([head, xr[8:]], axis=0)
        conv = conv + sh * cw_ref[CONV_K - 1 - back:CONV_K - back, :]
    act = _silu(conv)
    pc = pc_ref[...]
    beta = _sigmoid(pc)
    gl = -jnp.exp(alog_ref[...]) * _softplus(pc + dtb_ref[...])
    bb = _dot(beta, eb_ref[...], HI)
    gb = _dot(gl, eg_ref[...], HI)
    if n_valid < tm:
        live = lax.broadcasted_iota(jnp.int32, (tm, 1), 0) < n_valid
        act = jnp.where(live, act, 0.0)
        bb = jnp.where(live, bb, 0.0)
        gb = jnp.where(live, gb, 0.0)
    bb_ref[...] = bb
    gb_ref[...] = gb
    for h in range(C_HEADS):
        lo = h * HEAD_DIM
        qh = act[:, lo:lo + HEAD_DIM]
        kh = act[:, C_WIDTH + lo:C_WIDTH + lo + HEAD_DIM]
        qn = qh * lax.rsqrt(jnp.sum(qh * qh, axis=-1, keepdims=True) + EPS)
        q_ref[:, lo:lo + HEAD_DIM] = qn * (HEAD_DIM ** -0.5)
        k_ref[:, lo:lo + HEAD_DIM] = kh * lax.rsqrt(jnp.sum(kh * kh, axis=-1, keepdims=True) + EPS)
    v_ref[...] = act[:, 2 * C_WIDTH:]


def delta_prep(x_arr, x_colblk, xp_arr, xp_colblk, xp_rowmap, pc, conv_w, a_log, dt_bias, tm, tiles_per_seq, n_valid):
    m = pc.shape[0]
    w3 = 3 * C_WIDTH
    alog = jnp.zeros((1, 128), F32).at[0, 8:16].set(a_log)
    dtb = jnp.zeros((1, 128), F32).at[0, 8:16].set(dt_bias)
    lane_head = np.arange(C_WIDTH) // HEAD_DIM
    eb = jnp.asarray((np.arange(128)[:, None] == lane_head[None, :]).astype(np.float32))
    eg = jnp.asarray((np.arange(128)[:, None] == lane_head[None, :] + 8).astype(np.float32))
    full = lambda shape: pl.BlockSpec(shape, lambda i: (0, 0))
    o_shape = jax.ShapeDtypeStruct((m, C_WIDTH), F32)
    o_spec = pl.BlockSpec((tm, C_WIDTH), lambda i: (i, 0))
    return pl.pallas_call(
        functools.partial(_delta_prep_kernel, tm=tm, tiles_per_seq=tiles_per_seq, n_valid=n_valid),
        out_shape=[o_shape] * 5,
        grid=(m // tm,),
        in_specs=[pl.BlockSpec((tm, w3), lambda i: (i, x_colblk)),
                  pl.BlockSpec((8, w3), lambda i: (xp_rowmap(i), xp_colblk)),
                  pl.BlockSpec((tm, 128), lambda i: (i, 0)),
                  full((CONV_K, w3)), full((1, 128)), full((1, 128)), full((128, C_WIDTH)), full((128, C_WIDTH))],
        out_specs=[o_spec] * 5,
        compiler_params=_cp("parallel"),
        name="delta_prep",
    )(x_arr, xp_arr, pc, conv_w, alog, dtb, eb, eg)


def _dot_hi(a, b):
    return _dot(a, b, HI)


def _unit_lower_inverse(low, c, mm=_dot_hi):
    ii = lax.broadcasted_iota(jnp.int32, (c, c), 0)
    jj = lax.broadcasted_iota(jnp.int32, (c, c), 1)
    eye = (ii == jj).astype(F32)
    ld = jnp.where((ii >> 3) == (jj >> 3), low, 0.0)
    l2 = mm(ld, ld)
    l4 = mm(l2, l2)
    inv = mm(mm(eye - ld, eye + l2), eye + l4)
    s = 3
    while (1 << s) < c:
        off = ((ii >> (s + 1)) == (jj >> (s + 1))) & ((ii >> s) != (jj >> s))
        lo = jnp.where(off, low, 0.0)
        inv = inv - mm(mm(inv, lo), inv)
        s += 1
    return inv


def _split_bf16(x, n):
    parts, rest = [], x
    for _ in range(n):
        part = rest.astype(BF16)
        parts.append(part)
        rest = rest - part.astype(F32)
    return parts


def _dot_exact_lhs(a, b, nt=False):
    mm = _dot_nt if nt else _dot
    hi, mid, lo = _split_bf16(b, 3)
    return mm(a, hi) + (mm(a, mid) + mm(a, lo))


def _bdot(a, b):
    return lax.dot_general(a, b, (((2,), (1,)), ((0,), (0,))), preferred_element_type=F32)


def _bdot_hi(a, b):
    return lax.dot_general(a, b, (((2,), (1,)), ((0,), (0,))), precision=HI, preferred_element_type=F32)


def _bdot_nt(a, b):
    return lax.dot_general(a, b, (((2,), (2,)), ((0,), (0,))), preferred_element_type=F32)


def _bdot3(a, b):
    a_hi, a_lo = _split_bf16(a, 2)
    b_hi, b_lo = _split_bf16(b, 2)
    return _bdot(a_hi, b_hi) + (_bdot(a_hi, b_lo) + _bdot(a_lo, b_hi))


def _delta_wy_kernel(q_ref, k_ref, v_ref, bb_ref, gb_ref, u_ref, w_ref, at_ref, qg_ref, kd_ref, egl_ref, *, nch):
    c = CHUNK
    ii = lax.broadcasted_iota(jnp.int32, (c, c), 0)
    jj = lax.broadcasted_iota(jnp.int32, (c, c), 1)
    tri = (ii >= jj).astype(BF16)
    sel0 = (lax.broadcasted_iota(jnp.int32, (c, HEAD_DIM), 1) == 0).astype(BF16)
    heads = [slice(h * HEAD_DIM, (h + 1) * HEAD_DIM) for h in range(C_HEADS)]
    for t in range(nch):
        rows = slice(t * c, (t + 1) * c)
        q, k, v, bb = q_ref[rows, :], k_ref[rows, :], v_ref[rows, :], bb_ref[rows, :]
        gcb = _dot_exact_lhs(tri, gb_ref[rows, :])
        eg = jnp.exp(gcb)
        g_last = gcb[c - 1:c, :]
        kd_ref[rows, :] = (k * jnp.exp(g_last - gcb)).astype(BF16)
        qg_ref[rows, :] = (q * eg).astype(BF16)
        egl_ref[t] = jnp.broadcast_to(jnp.exp(g_last), (8, C_WIDTH))
        kb = k * bb
        vb = v * bb
        kbe = kb * eg
        gcr = jnp.stack([_dot_exact_lhs(sel0, gcb[:, hs], nt=True) for hs in heads])
        gci = jnp.stack([gcb[:, hs][:, :c] for hs in heads])
        dec = jnp.where(ii >= jj, jnp.exp(jnp.minimum(gci - gcr, 0.0)), 0.0)
        lhs = jnp.stack([jnp.concatenate([kb[:, hs], q[:, hs]], axis=0) for hs in heads]).astype(BF16)
        qk = _bdot_nt(lhs, jnp.stack([k[:, hs] for hs in heads]).astype(BF16))
        low = jnp.where(ii > jj, qk[:, :c] * dec, 0.0)
        attn = qk[:, c:] * dec
        tmat = _unit_lower_inverse(low, c, _bdot3)
        rhs = jnp.stack([jnp.concatenate([vb[:, hs], kbe[:, hs]], axis=1) for hs in heads]).astype(BF16)
        uw = _bdot(tmat.astype(BF16), rhs)
        for h, hs in enumerate(heads):
            u_ref[rows, hs] = uw[h, :, :HEAD_DIM]
            w_ref[rows, hs] = uw[h, :, HEAD_DIM:].astype(BF16)
            at_ref[rows, h * c:(h + 1) * c] = attn[h].astype(BF16)


def delta_wy(q, k, v, bb, gb, nch):
    m = q.shape[0]
    rows = nch * CHUNK
    row = pl.BlockSpec((rows, C_WIDTH), lambda i: (i, 0))
    sds = lambda w, dt: jax.ShapeDtypeStruct((m, w), dt)
    return pl.pallas_call(
        functools.partial(_delta_wy_kernel, nch=nch),
        out_shape=[sds(C_WIDTH, F32), sds(C_WIDTH, BF16), sds(C_HEADS * CHUNK, BF16), sds(C_WIDTH, BF16),
                   sds(C_WIDTH, BF16), jax.ShapeDtypeStruct((m // CHUNK, 8, C_WIDTH), F32)],
        grid=(m // rows,),
        in_specs=[row] * 5,
        out_specs=[row, row, pl.BlockSpec((rows, C_HEADS * CHUNK), lambda i: (i, 0)), row, row,
                   pl.BlockSpec((nch, 8, C_WIDTH), lambda i: (i, 0, 0))],
        compiler_params=_cp("parallel"),
        name="delta_wy",
    )(q, k, v, bb, gb)


def _delta_rec_kernel(u_ref, w_ref, at_ref, qg_ref, kd_ref, egl_ref, z_ref, s0_ref, nw_ref, o_ref, sout_ref, s_ref,
                      *, n_seq):
    ci = pl.program_id(0)
    c = CHUNK

    @pl.when(ci == 0)
    def _():
        s_ref[...] = s0_ref[...]

    nw = nw_ref[...]
    for n in range(n_seq):
        idx = [(n * C_HEADS + h, slice(h * HEAD_DIM, (h + 1) * HEAD_DIM)) for h in range(C_HEADS)]
        st = [s_ref[i] for i, _ in idx]
        r = [_dot(jnp.concatenate([w_ref[n, :, hs], qg_ref[n, :, hs]], axis=0), s.astype(BF16))
             for (_, hs), s in zip(idx, st)]
        vn = [(u_ref[n, :, hs] - rr[:c]).astype(BF16) for (_, hs), rr in zip(idx, r)]
        for h, ((i, hs), s, rr, vb) in enumerate(zip(idx, st, r, vn)):
            o = rr[c:] + _dot(at_ref[n, :, h * c:(h + 1) * c], vb)
            s_ref[i] = s * egl_ref[n, 0, 0:1, hs] + _dot_tn(kd_ref[n, :, hs], vb)
            on = o * lax.rsqrt(jnp.mean(o * o, axis=-1, keepdims=True) + EPS) * nw * _silu(z_ref[n, :, hs])
            o_ref[n, :, hs] = on.astype(o_ref.dtype)

    @pl.when(ci == pl.num_programs(0) - 1)
    def _():
        sout_ref[...] = s_ref[...]


def delta_rec(u, w, at, qg, kd, egl, z_arr, z_colblk, s0, norm_w, n_seq, seq):
    nc = seq // CHUNK
    v3 = lambda x: x.reshape(n_seq, seq, x.shape[-1])
    row = pl.BlockSpec((n_seq, CHUNK, C_WIDTH), lambda t: (0, t, 0))
    s_spec = pl.BlockSpec((n_seq * C_HEADS, HEAD_DIM, HEAD_DIM), lambda t: (0, 0, 0))
    o, s_fin = pl.pallas_call(
        functools.partial(_delta_rec_kernel, n_seq=n_seq),
        out_shape=[jax.ShapeDtypeStruct((n_seq, seq, C_WIDTH), BF16),
                   jax.ShapeDtypeStruct((n_seq * C_HEADS, HEAD_DIM, HEAD_DIM), F32)],
        grid=(nc,),
        in_specs=[row, row, pl.BlockSpec((n_seq, CHUNK, C_HEADS * CHUNK), lambda t: (0, t, 0)), row, row,
                  pl.BlockSpec((n_seq, 1, 8, C_WIDTH), lambda t: (0, t, 0, 0)),
                  pl.BlockSpec((n_seq, CHUNK, C_WIDTH), lambda t: (0, t, z_colblk)),
                  s_spec, pl.BlockSpec((1, HEAD_DIM), lambda t: (0, 0))],
        out_specs=[row, s_spec],
        scratch_shapes=[pltpu.VMEM((n_seq * C_HEADS, HEAD_DIM, HEAD_DIM), F32)],
        compiler_params=_cp("arbitrary"),
        name="delta_rec",
    )(v3(u), v3(w), v3(at), v3(qg), v3(kd), egl.reshape(n_seq, nc, 8, C_WIDTH), v3(z_arr),
      s0.reshape(n_seq * C_HEADS, HEAD_DIM, HEAD_DIM), norm_w.reshape(1, HEAD_DIM))
    return o.reshape(n_seq * seq, C_WIDTH), s_fin.reshape(n_seq, C_HEADS, HEAD_DIM, HEAD_DIM)


def _delta_chunk_kernel(q_ref, k_ref, v_ref, bb_ref, gb_ref, z_ref, s0_ref, nw_ref, o_ref, sout_ref, s_ref, *, c):
    ci = pl.program_id(1)

    @pl.when(ci == 0)
    def _():
        s_ref[...] = s0_ref[0]

    ii = lax.broadcasted_iota(jnp.int32, (c, c), 0)
    jj = lax.broadcasted_iota(jnp.int32, (c, c), 1)
    tri = (ii >= jj).astype(F32)
    sel0 = (lax.broadcasted_iota(jnp.int32, (c, HEAD_DIM), 1) == 0).astype(F32)
    nw = nw_ref[...]
    heads = [slice(h * HEAD_DIM, (h + 1) * HEAD_DIM) for h in range(C_HEADS)]
    q = [q_ref[:, hs] for hs in heads]
    k = [k_ref[:, hs] for hs in heads]
    bb = [bb_ref[:, hs] for hs in heads]
    gcb = [_dot(tri, gb_ref[:, hs], HI) for hs in heads]
    gcr = [_dot_nt(sel0, g, HI) for g in gcb]
    dec = [jnp.where(ii >= jj, jnp.exp(jnp.minimum(g[:, :c] - r, 0.0)), 0.0) for g, r in zip(gcb, gcr)]
    kb = [kh * bh for kh, bh in zip(k, bb)]
    low = jnp.stack([jnp.where(ii > jj, _dot_nt(kbh, kh, HI) * d, 0.0) for kbh, kh, d in zip(kb, k, dec)])
    tmat = _unit_lower_inverse(low, c, _bdot_hi)
    eg = [jnp.exp(g) for g in gcb]
    u = [_dot(tmat[h], v_ref[:, hs] * bb[h], HI) for h, hs in enumerate(heads)]
    w = [_dot(tmat[h], kb[h] * eg[h], HI) for h in range(C_HEADS)]
    s = [s_ref[h] for h in range(C_HEADS)]
    v_new = [uh - _dot(wh, sh, HI) for uh, wh, sh in zip(u, w, s)]
    attn = [_dot_nt(qh, kh, HI) * d for qh, kh, d in zip(q, k, dec)]
    o = [_dot(qh * e, sh, HI) + _dot(a, vn, HI) for qh, e, sh, a, vn in zip(q, eg, s, attn, v_new)]
    for h, hs in enumerate(heads):
        g_last = gcb[h][c - 1:c, :]
        s_ref[h] = s[h] * jnp.exp(g_last) + _dot_tn(k[h] * jnp.exp(g_last - gcb[h]), v_new[h], HI)
        on = o[h] * lax.rsqrt(jnp.mean(o[h] * o[h], axis=-1, keepdims=True) + EPS) * nw * _silu(z_ref[:, hs])
        o_ref[:, hs] = on.astype(o_ref.dtype)

    @pl.when(ci == pl.num_programs(1) - 1)
    def _():
        sout_ref[0] = s_ref[...]


def delta_chunks(q, k, v, bb, gb, z_arr, z_colblk, s0, s0_off, norm_w, n_seq, c, out_dtype):
    m = q.shape[0]
    nc = m // n_seq // c
    row = pl.BlockSpec((c, C_WIDTH), lambda n, t: (n * nc + t, 0))
    s_spec = pl.BlockSpec((1, C_HEADS, HEAD_DIM, HEAD_DIM), lambda n, t: (n, 0, 0, 0))
    s0_spec = pl.BlockSpec((1, C_HEADS, HEAD_DIM, HEAD_DIM), lambda n, t: (s0_off + n, 0, 0, 0))
    return pl.pallas_call(
        functools.partial(_delta_chunk_kernel, c=c),
        out_shape=[jax.ShapeDtypeStruct((m, C_WIDTH), out_dtype),
                   jax.ShapeDtypeStruct((n_seq, C_HEADS, HEAD_DIM, HEAD_DIM), F32)],
        grid=(n_seq, nc),
        in_specs=[row, row, row, row, row,
                  pl.BlockSpec((c, C_WIDTH), lambda n, t: (n * nc + t, z_colblk)),
                  s0_spec, pl.BlockSpec((1, HEAD_DIM), lambda n, t: (0, 0))],
        out_specs=[row, s_spec],
        scratch_shapes=[pltpu.VMEM((C_HEADS, HEAD_DIM, HEAD_DIM), F32)],
        compiler_params=_cp("parallel", "arbitrary"),
        name="delta_chunks",
    )(q, k, v, bb, gb, z_arr, s0, norm_w.reshape(1, HEAD_DIM))


def _merge_kernel(oa_ref, ob_ref, oc_ref, ga_ref, gb_ref, gc_ref, wa_ref, wb_ref, wc_ref, o_ref, *, prec):
    ya = _dot(oa_ref[...], wa_ref[...], prec)
    yb = _dot(ob_ref[...], wb_ref[...], prec)
    yc = _dot(oc_ref[...], wc_ref[...], prec)
    t = _sigmoid(ga_ref[...]) * ya + _sigmoid(gb_ref[...]) * yb + _sigmoid(gc_ref[...]) * yc
    o_ref[...] = t.astype(o_ref.dtype)


def merge_branches(oa, ob, oc, p, w_a, w_b, w_c, tm, tn):
    m = oa.shape[0]
    gate0 = 10240 // tn
    per = D_MODEL // tn
    o_spec = pl.BlockSpec((tm, 1024), lambda i, j: (i, 0))
    w_spec = pl.BlockSpec((1024, tn), lambda i, j: (0, j))
    gate_spec = lambda b: pl.BlockSpec((tm, tn), lambda i, j: (i, gate0 + b * per + j))
    return pl.pallas_call(
        functools.partial(_merge_kernel, prec=_prec(w_a)),
        out_shape=jax.ShapeDtypeStruct((m, D_MODEL), w_a.dtype),
        grid=(m // tm, D_MODEL // tn),
        in_specs=[o_spec, o_spec, o_spec, gate_spec(0), gate_spec(1), gate_spec(2), w_spec, w_spec, w_spec],
        out_specs=pl.BlockSpec((tm, tn), lambda i, j: (i, j)),
        compiler_params=_cp("parallel", "arbitrary"),
        name="merge_branches",
    )(oa, ob, oc, p, p, p, w_a, w_b, w_c)


def _router_kernel(x_ref, g_ref, r_ref, h_ref, sel_ref):
    x = x_ref[...]
    hn = x * lax.rsqrt(jnp.mean(x * x, axis=-1, keepdims=True) + EPS) * g_ref[...]
    h_ref[...] = hn
    logits = _dot(hn, r_ref[...], HI)
    lane_i = lax.broadcasted_iota(jnp.int32, logits.shape, 1)
    lane = lane_i.astype(F32)
    logits = jnp.where(lane_i < N_EXPERTS, logits, NEG)
    v1 = jnp.max(logits, axis=-1, keepdims=True)
    i1 = jnp.min(jnp.where(logits == v1, lane, 128.0), axis=-1, keepdims=True)
    rest = jnp.where(lane == i1, NEG, logits)
    v2 = jnp.max(rest, axis=-1, keepdims=True)
    i2 = jnp.min(jnp.where(rest == v2, lane, 128.0), axis=-1, keepdims=True)
    e = jnp.exp(v2 - v1)
    tot = 1.0 + e
    sel_ref[...] = jnp.where(lane_i == 0, i1, jnp.where(lane_i == 1, i2, jnp.where(lane_i == 2, 1.0 / tot, e / tot)))


def router(x, g, router_w, tm):
    m, d = x.shape
    rw = jnp.zeros((d, 128), F32).at[:, :N_EXPERTS].set(router_w)
    return pl.pallas_call(
        _router_kernel,
        out_shape=[jax.ShapeDtypeStruct((m, d), F32), jax.ShapeDtypeStruct((m, 128), F32)],
        grid=(m // tm,),
        in_specs=[pl.BlockSpec((tm, d), lambda i: (i, 0)), pl.BlockSpec((1, d), lambda i: (0, 0)),
                  pl.BlockSpec((d, 128), lambda i: (0, 0))],
        out_specs=[pl.BlockSpec((tm, d), lambda i: (i, 0)), pl.BlockSpec((tm, 128), lambda i: (i, 0))],
        compiler_params=_cp("parallel"),
        name="router",
    )(x, g.reshape(1, d), rw)


MOE_TM = 256


def _moe_plan(sel, m):
    i1 = sel[:, 0].astype(jnp.int32)
    i2 = sel[:, 1].astype(jnp.int32)
    e_flat = jnp.concatenate([i1, i2])
    tok = jnp.tile(jnp.arange(m, dtype=jnp.int32), 2)
    onehot = (e_flat[:, None] == jnp.arange(N_EXPERTS, dtype=jnp.int32)[None, :]).astype(jnp.int32)
    counts = jnp.sum(onehot, axis=0)
    rank = jnp.take_along_axis(jnp.cumsum(onehot, axis=0) - onehot, e_flat[:, None], axis=1)[:, 0]
    padded = (counts + MOE_TM - 1) // MOE_TM * MOE_TM
    ends = jnp.cumsum(padded)
    pos = (ends - padded)[e_flat] + rank
    n_tiles = (2 * m) // MOE_TM + N_EXPERTS
    row_token = jnp.zeros((n_tiles * MOE_TM,), jnp.int32).at[pos].set(tok)
    tiles = jnp.arange(n_tiles, dtype=jnp.int32)
    n_used = ends[-1] // MOE_TM
    tile_src = jnp.minimum(tiles, n_used - 1)
    tile_expert = jnp.sum((tile_src * MOE_TM)[:, None] >= ends[None, :], axis=1).astype(jnp.int32)
    tile_used = (tiles < n_used).astype(jnp.int32)
    return row_token, pos[:m], pos[m:], tile_expert, tile_src, tile_used


def _moe_gather_kernel(tok_ref, used_ref, h_hbm, o_ref, buf, sem):
    i = pl.program_id(0)

    @pl.when(used_ref[i] == 1)
    def _():
        def issue(r, carry):
            pltpu.make_async_copy(h_hbm.at[pl.ds(tok_ref[i * MOE_TM + r], 1)], buf.at[pl.ds(r, 1)], sem).start()
            return carry
        lax.fori_loop(0, MOE_TM, issue, 0)
        pltpu.make_async_copy(h_hbm.at[pl.ds(0, MOE_TM)], buf, sem).wait()
        o_ref[...] = buf[...].astype(o_ref.dtype)

    @pl.when(used_ref[i] == 0)
    def _():
        o_ref[...] = jnp.zeros_like(o_ref)


def moe_gather(h, row_token, tile_used):
    d = h.shape[1]
    n_tiles = tile_used.shape[0]
    return pl.pallas_call(
        _moe_gather_kernel,
        out_shape=jax.ShapeDtypeStruct((n_tiles * MOE_TM, d), BF16),
        grid_spec=pltpu.PrefetchScalarGridSpec(
            num_scalar_prefetch=2,
            grid=(n_tiles,),
            in_specs=[pl.BlockSpec(memory_space=pl.ANY)],
            out_specs=pl.BlockSpec((MOE_TM, d), lambda i, tok, used: (i, 0)),
            scratch_shapes=[pltpu.VMEM((MOE_TM, d), F32), pltpu.SemaphoreType.DMA(())]),
        compiler_params=_cp("arbitrary"),
        name="moe_gather",
    )(row_token, tile_used, h)


def _moe_up_kernel(te_ref, ts_ref, tu_ref, h_ref, wg_ref, wu_ref, o_ref, wg_bf, wu_bf):
    t = pl.program_id(1)
    used = tu_ref[t] == 1
    new_block = jnp.logical_or(t == 0, te_ref[t] != te_ref[jnp.maximum(t - 1, 0)])

    @pl.when(jnp.logical_and(used, new_block))
    def _():
        wg_bf[...] = wg_ref[0].astype(BF16)
        wu_bf[...] = wu_ref[0].astype(BF16)

    @pl.when(used)
    def _():
        h = h_ref[...]
        o_ref[...] = (_silu(_dot(h, wg_bf[...])) * _dot(h, wu_bf[...])).astype(o_ref.dtype)

    @pl.when(jnp.logical_not(used))
    def _():
        o_ref[...] = jnp.zeros_like(o_ref)


def moe_up(hs, wg, wu, plan, tn):
    tile_expert, tile_src, tile_used = plan
    d, f = wg.shape[1], wg.shape[2]
    n_tiles = tile_used.shape[0]
    w_spec = pl.BlockSpec((1, d, tn), lambda j, t, te, ts, tu: (te[t], 0, j))
    return pl.pallas_call(
        _moe_up_kernel,
        out_shape=jax.ShapeDtypeStruct((n_tiles * MOE_TM, f), BF16),
        grid_spec=pltpu.PrefetchScalarGridSpec(
            num_scalar_prefetch=3,
            grid=(f // tn, n_tiles),
            in_specs=[pl.BlockSpec((MOE_TM, d), lambda j, t, te, ts, tu: (ts[t], 0)), w_spec, w_spec],
            out_specs=pl.BlockSpec((MOE_TM, tn), lambda j, t, te, ts, tu: (t, j)),
            scratch_shapes=[pltpu.VMEM((d, tn), BF16), pltpu.VMEM((d, tn), BF16)]),
        compiler_params=_cp("arbitrary", "arbitrary"),
        name="moe_up",
    )(tile_expert, tile_src, tile_used, hs, wg, wu)


def _moe_down_kernel(te_ref, ts_ref, tu_ref, a_ref, w_ref, o_ref, w_bf):
    t = pl.program_id(1)
    used = tu_ref[t] == 1
    new_block = jnp.logical_or(t == 0, te_ref[t] != te_ref[jnp.maximum(t - 1, 0)])

    @pl.when(jnp.logical_and(used, new_block))
    def _():
        w_bf[...] = w_ref[0].astype(BF16)

    @pl.when(used)
    def _():
        o_ref[...] = _dot(a_ref[...], w_bf[...])

    @pl.when(jnp.logical_not(used))
    def _():
        o_ref[...] = jnp.zeros_like(o_ref)


def moe_down(act, wd, plan, tn):
    tile_expert, tile_src, tile_used = plan
    f, d = wd.shape[1], wd.shape[2]
    n_tiles = tile_used.shape[0]
    return pl.pallas_call(
        _moe_down_kernel,
        out_shape=jax.ShapeDtypeStruct((n_tiles * MOE_TM, d), F32),
        grid_spec=pltpu.PrefetchScalarGridSpec(
            num_scalar_prefetch=3,
            grid=(d // tn, n_tiles),
            in_specs=[pl.BlockSpec((MOE_TM, f), lambda j, t, te, ts, tu: (ts[t], 0)),
                      pl.BlockSpec((1, f, tn), lambda j, t, te, ts, tu: (te[t], 0, j))],
            out_specs=pl.BlockSpec((MOE_TM, tn), lambda j, t, te, ts, tu: (t, j)),
            scratch_shapes=[pltpu.VMEM((f, tn), BF16)]),
        compiler_params=_cp("arbitrary", "arbitrary"),
        name="moe_down",
    )(tile_expert, tile_src, tile_used, act, wd)


def _moe_combine_kernel(p1_ref, p2_ref, x_ref, sel_ref, y_hbm, o_ref, b1, b2, sems, *, tm):
    base = pl.program_id(0) * tm

    def issue(r, carry):
        pltpu.make_async_copy(y_hbm.at[pl.ds(p1_ref[base + r], 1)], b1.at[pl.ds(r, 1)], sems.at[0]).start()
        pltpu.make_async_copy(y_hbm.at[pl.ds(p2_ref[base + r], 1)], b2.at[pl.ds(r, 1)], sems.at[1]).start()
        return carry
    lax.fori_loop(0, tm, issue, 0)
    pltpu.make_async_copy(y_hbm.at[pl.ds(0, tm)], b1, sems.at[0]).wait()
    pltpu.make_async_copy(y_hbm.at[pl.ds(0, tm)], b2, sems.at[1]).wait()
    o_ref[...] = x_ref[...] + (sel_ref[:, 2:3] * b1[...] + sel_ref[:, 3:4] * b2[...])


def moe_combine(x, sel, y, pos1, pos2, tm):
    m, d = x.shape
    return pl.pallas_call(
        functools.partial(_moe_combine_kernel, tm=tm),
        out_shape=jax.ShapeDtypeStruct((m, d), F32),
        grid_spec=pltpu.PrefetchScalarGridSpec(
            num_scalar_prefetch=2,
            grid=(m // tm,),
            in_specs=[pl.BlockSpec((tm, d), lambda i, p1, p2: (i, 0)),
                      pl.BlockSpec((tm, 128), lambda i, p1, p2: (i, 0)),
                      pl.BlockSpec(memory_space=pl.ANY)],
            out_specs=pl.BlockSpec((tm, d), lambda i, p1, p2: (i, 0)),
            scratch_shapes=[pltpu.VMEM((tm, d), F32), pltpu.VMEM((tm, d), F32), pltpu.SemaphoreType.DMA((2,))]),
        compiler_params=_cp("arbitrary"),
        name="moe_combine",
    )(pos1, pos2, x, sel, y)


def moe_ffn(x, g, router_w, wg, wu, wd):
    m = x.shape[0]
    h, sel = router(x, g, router_w, 320)
    row_token, pos1, pos2, tile_expert, tile_src, tile_used = _moe_plan(sel, m)
    plan = (tile_expert, tile_src, tile_used)
    hs = moe_gather(h, row_token, tile_used)
    act = moe_up(hs, wg, wu, plan, 1024)
    y = moe_down(act, wd, plan, 512)
    return moe_combine(x, sel, y, pos1, pos2, 128)


def _split_w_in(w):
    qa, ka, va, qb, kb, vb, cqkv, z, cb, ca, gates = jnp.split(w, np.cumsum(PROJ_SIZES)[:-1].tolist(), axis=-1)
    main = jnp.concatenate([qa, qb, ka, va, kb, vb, cqkv, z, gates], axis=-1)
    small = jnp.concatenate([cb, ca, jnp.zeros((w.shape[0], 112), w.dtype)], axis=-1)
    return main, small


def _mix_prompt(h, lp):
    w_main, w_small, sink, conv_w, a_log, dt_bias, norm_w, w_a, w_b, w_c = lp
    m = h.shape[0]
    p = matmul(h, w_main, 2048, 512)
    pc = matmul(h, w_small, 512, 128)
    p3 = p.reshape(BATCH, SEQ, PW)
    oa = window_attention(p3, _alibi(A_HEADS), sink)
    ob = dilated_attention(p3, _alibi(N_BGROUPS * B_HPG).reshape(N_BGROUPS, B_HPG))
    tm = 256
    q, k, v, bb, gb = delta_prep(p, 2, p, 2, lambda i: jnp.maximum(i * (tm // 8) - 1, 0), pc, conv_w, a_log,
                                 dt_bias, tm, SEQ // tm, tm)
    s0 = jnp.zeros((BATCH, C_HEADS, HEAD_DIM, HEAD_DIM), F32)
    u, w, at, qg, kd, egl = delta_wy(q, k, v, bb, gb, 2)
    oc, s_fin = delta_rec(u, w, at, qg, kd, egl, p, 9, s0, norm_w, BATCH, SEQ)
    t = merge_branches(oa, ob, oc, p, w_a, w_b, w_c, 1024, 512)
    p4 = p.reshape(BATCH, SEQ, PW)
    kv = lambda blk, win: p4[:, SEQ - win:, blk * 256:(blk + 1) * 256].reshape(BATCH, win, 2, HEAD_DIM)
    state = [kv(16, A_WINDOW), kv(17, A_WINDOW)]
    for gi, (win, _) in enumerate(B_GROUPS):
        state += [kv(18 + gi, win), kv(21 + gi, win)]
    state += [p4[:, SEQ - (CONV_K - 1):, 6144:9216], s_fin]
    return t, state


def _mix_sample(h, lp, caches, layer):
    w_main, w_small, sink, conv_w, a_log, dt_bias, norm_w, w_a, w_b, w_c = lp
    c_ak, c_av, b1k, b1v, b2k, b2v, b3k, b3v, conv_all, s0_all = caches
    nb, nt = DEC_BATCH, DEC_SEQ
    m = nb * nt
    p = matmul(h, w_main, m, 512)
    pc = matmul(h, w_small, m, 128)
    p3 = p.reshape(nb, nt, PW)

    def heads_q(blk):
        qq = p3[:, :, blk * 1024:(blk + 1) * 1024].reshape(nb, nt, 2, 4, HEAD_DIM)
        return jnp.transpose(qq, (0, 2, 3, 1, 4)).reshape(nb, 2, 16, HEAD_DIM)

    def heads_o(o):
        oo = o.reshape(nb, 2, 4, nt, HEAD_DIM)
        return jnp.transpose(oo, (0, 3, 1, 2, 4)).reshape(m, 1024)

    def new_rows(blk):
        x = p3[:, :, blk * 256:(blk + 1) * 256]
        return x.reshape(nb, nt, 2, HEAD_DIM), jnp.pad(x, ((0, 0), (0, DEC_PAD - nt), (0, 0)))

    def reach(buf, dil):
        win, keep = buf.shape[2], min(dil, nt)
        x = buf.reshape(DEPTH * nb, win // dil, dil, 2, HEAD_DIM)[:, :, :keep]
        return x.reshape(DEPTH * nb, win // dil * keep, 256), win

    ka, ka_p = new_rows(16)
    va, va_p = new_rows(17)
    oa, _ = window_decode(heads_q(0), reach(c_ak, 1)[0], reach(c_av, 1)[0], layer, A_WINDOW, ka_p, va_p,
                          _alibi(A_HEADS), 1, sink)
    oa = heads_o(oa)
    appended = [ka, va]
    slopes_b = _alibi(N_BGROUPS * B_HPG).reshape(N_BGROUPS, B_HPG)
    outs, lses = [], []
    for gi, ((_, dil), (bk, bv)) in enumerate(zip(B_GROUPS, ((b1k, b1v), (b2k, b2v), (b3k, b3v)))):
        kn, kn_p = new_rows(18 + gi)
        vn, vn_p = new_rows(21 + gi)
        (rk, win), (rv, _) = reach(bk, dil), reach(bv, dil)
        o, lse = window_decode(heads_q(1 + gi), rk, rv, layer, win, kn_p, vn_p, slopes_b[gi], dil, None)
        outs.append(heads_o(o))
        lses.append(heads_o(lse))
        appended += [kn, vn]
    ob = combine_groups(outs, lses, m, F32)

    pad_rows = lambda x: jnp.pad(x.reshape(nb, nt, -1), ((0, 0), (0, DEC_PAD - nt), (0, 0))).reshape(nb * DEC_PAD, -1)
    cq = p3[:, :, 6144:9216]
    xs = pad_rows(cq)
    xprev = jnp.pad(conv_all[layer], ((0, 0), (8 - (CONV_K - 1), 0), (0, 0))).reshape(nb * 8, 3 * C_WIDTH)
    q, k, v, bb, gb = delta_prep(xs, 0, xprev, 0, lambda i: i, pad_rows(pc), conv_w, a_log, dt_bias,
                                 DEC_PAD, None, nt)
    zs = pad_rows(p3[:, :, 9216:10240])
    s0_flat = s0_all.reshape(DEPTH * nb, C_HEADS, HEAD_DIM, HEAD_DIM)
    oc, s_fin = delta_chunks(q, k, v, bb, gb, zs, 0, s0_flat, layer * nb, norm_w, nb, DEC_PAD, F32)
    oc = oc.reshape(nb, DEC_PAD, C_WIDTH)[:, :nt].reshape(m, C_WIDTH)
    t = merge_branches(oa, ob, oc, p, w_a, w_b, w_c, m, 512)
    appended += [cq, s_fin]
    return t, appended


def kernel(x_prompt, x_sample, cache_a_k, cache_a_v, cache_b1_k, cache_b1_v, cache_b2_k, cache_b2_v, cache_b3_k, cache_b3_v, state_c_conv, state_c_rec, norm_mix, norm_ffn, norm_final, w_in, attn_sink, conv_w, a_log, dt_bias, norm_delta, w_out_a, w_out_b, w_out_c, w_out, ffn_w_gate, ffn_w_up, ffn_w_down, router_w, moe_w_gate, moe_w_up, moe_w_down):
    mp, ms = BATCH * SEQ, DEC_BATCH * DEC_SEQ
    xp = x_prompt.reshape(mp, D_MODEL)
    xs = x_sample.reshape(ms, D_MODEL)
    tmp = 1024
    p_states, s_states = [], []
    caches = (cache_a_k, cache_a_v, cache_b1_k, cache_b1_v, cache_b2_k, cache_b2_v, cache_b3_k, cache_b3_v,
              state_c_conv, state_c_rec)
    for l in range(DEPTH):
        w_main, w_small = _split_w_in(w_in[l])
        shared = (attn_sink[l], conv_w[l], a_log[l], dt_bias[l], norm_delta[l])
        lp_s = (w_main, w_small, *shared, w_out_a[l], w_out_b[l], w_out_c[l])
        lp_p = (w_main.astype(BF16), w_small.astype(BF16), *shared,
                w_out_a[l].astype(BF16), w_out_b[l].astype(BF16), w_out_c[l].astype(BF16))
        tp, sp = _mix_prompt(rmsnorm(xp, norm_mix[l], BF16, tmp), lp_p)
        ts, ss = _mix_sample(rmsnorm(xs, norm_mix[l], F32, ms), lp_s, caches, l)
        xp = matmul_residual(xp, tp, w_out[l].astype(BF16), tmp, 512)
        xs = matmul_residual(xs, ts, w_out[l], ms, 512)
        p_states.append(sp)
        s_states.append(ss)
        i = l // 2
        if l % 2 == 0:
            wg, wu, wd = ffn_w_gate[i], ffn_w_up[i], ffn_w_down[i]
            hp = rmsnorm(xp, norm_ffn[l], BF16, tmp)
            hs = rmsnorm(xs, norm_ffn[l], F32, ms)
            xp = matmul_residual(xp, swiglu_up(hp, wg.astype(BF16), wu.astype(BF16), tmp, 512), wd.astype(BF16),
                                 tmp, 512)
            xs = matmul_residual(xs, swiglu_up(hs, wg, wu, ms, 512), wd, ms, 512)
        else:
            x_all = moe_ffn(jnp.concatenate([xp, xs], axis=0), norm_ffn[l], router_w[i], moe_w_gate[i],
                            moe_w_up[i], moe_w_down[i])
            xp, xs = x_all[:mp], x_all[mp:]
    y_prompt = rmsnorm(xp, norm_final, F32, tmp).reshape(BATCH, SEQ, D_MODEL)
    y_sample = rmsnorm(xs, norm_final, F32, ms).reshape(DEC_BATCH, DEC_SEQ, D_MODEL)
    p_out = [jnp.stack(z) for z in zip(*p_states)]
    s_new = [jnp.stack(z) for z in zip(*s_states)]
    s_out = [jnp.concatenate([old, new], axis=2)[:, :, DEC_SEQ:] for old, new in zip(caches[:-1], s_new[:-1])]
    s_out.append(s_new[-1])
    return (y_prompt, y_sample, *p_out, *s_out)
```

```python
import functools

import numpy as np
import jax
import jax.numpy as jnp
from jax import lax
from jax.experimental import pallas as pl
from jax.experimental.pallas import tpu as pltpu

F32 = jnp.float32
BF16 = jnp.bfloat16
HI = lax.Precision.HIGHEST

D_MODEL = 2048
BATCH = 4
SEQ = 2048
DEPTH = 2
DEC_BATCH = 32
DEC_SEQ = 4
HEAD_DIM = 128
A_HEADS = 8
A_KV = 2
A_WINDOW = 128
B_GROUPS = ((128, 1), (512, 4), (2048, 16))
N_BGROUPS = 3
B_HPG = 8
B_KV = 2
C_HEADS = 8
C_WIDTH = C_HEADS * HEAD_DIM
CONV_K = 4
CHUNK = 64
BAND = 128
D_FF = 5632
N_EXPERTS = 8
D_FF_EXPERT = 7168
PROJ_SIZES = (1024, 256, 256, 3072, 768, 768, 3072, 1024, 8, 8, 6144)
EPS = 1e-6
NEG = -1e30

PW = 16384
DEC_PAD = 8
VMEM_LIMIT_BYTES = 56 * 1024 * 1024


def _cp(*sem):
    return pltpu.CompilerParams(dimension_semantics=sem, vmem_limit_bytes=VMEM_LIMIT_BYTES)


def _alibi(n):
    return np.asarray(2.0 ** (-8.0 * np.arange(1, n + 1) / n), dtype=np.float32)


def _sigmoid(x):
    return 1.0 / (1.0 + jnp.exp(-x))


def _silu(x):
    return x * _sigmoid(x)


def _softplus(x):
    return jnp.maximum(x, 0.0) + jnp.log(1.0 + jnp.exp(-jnp.abs(x)))


def _dot(a, b, prec=None):
    return jnp.dot(a, b, precision=prec, preferred_element_type=F32)


def _dot_nt(a, b, prec=None):
    return lax.dot_general(a, b, (((1,), (1,)), ((), ())), precision=prec, preferred_element_type=F32)


def _dot_tn(a, b, prec=None):
    return lax.dot_general(a, b, (((0,), (0,)), ((), ())), precision=prec, preferred_element_type=F32)


def _rmsnorm_kernel(x_ref, g_ref, o_ref):
    x = x_ref[...]
    ms = jnp.mean(x * x, axis=-1, keepdims=True)
    o_ref[...] = (x * lax.rsqrt(ms + EPS) * g_ref[...]).astype(o_ref.dtype)


def rmsnorm(x, g, out_dtype, tm):
    m, d = x.shape
    return pl.pallas_call(
        _rmsnorm_kernel,
        out_shape=jax.ShapeDtypeStruct((m, d), out_dtype),
        grid=(m // tm,),
        in_specs=[pl.BlockSpec((tm, d), lambda i: (i, 0)), pl.BlockSpec((1, d), lambda i: (0, 0))],
        out_specs=pl.BlockSpec((tm, d), lambda i: (i, 0)),
        compiler_params=_cp("parallel"),
        name="rmsnorm",
    )(x, g.reshape(1, d))


def _prec(x):
    return HI if x.dtype == F32 else None


def _mm_kernel(a_ref, w_ref, o_ref, *, prec):
    o_ref[...] = _dot(a_ref[...], w_ref[...], prec).astype(o_ref.dtype)


def matmul(a, w, tm, tn, out_dtype=F32):
    m, k = a.shape
    n = w.shape[1]
    return pl.pallas_call(
        functools.partial(_mm_kernel, prec=_prec(w)),
        out_shape=jax.ShapeDtypeStruct((m, n), out_dtype),
        grid=(m // tm, n // tn),
        in_specs=[pl.BlockSpec((tm, k), lambda i, j: (i, 0)), pl.BlockSpec((k, tn), lambda i, j: (0, j))],
        out_specs=pl.BlockSpec((tm, tn), lambda i, j: (i, j)),
        compiler_params=_cp("parallel", "arbitrary"),
        name="matmul",
    )(a, w)


def _mm_res_kernel(x_ref, a_ref, w_ref, o_ref, *, prec):
    o_ref[...] = x_ref[...] + _dot(a_ref[...], w_ref[...], prec)


def matmul_residual(x, a, w, tm, tn):
    m, k = a.shape
    n = w.shape[1]
    return pl.pallas_call(
        functools.partial(_mm_res_kernel, prec=_prec(w)),
        out_shape=jax.ShapeDtypeStruct((m, n), F32),
        grid=(m // tm, n // tn),
        in_specs=[pl.BlockSpec((tm, tn), lambda i, j: (i, j)),
                  pl.BlockSpec((tm, k), lambda i, j: (i, 0)),
                  pl.BlockSpec((k, tn), lambda i, j: (0, j))],
        out_specs=pl.BlockSpec((tm, tn), lambda i, j: (i, j)),
        compiler_params=_cp("parallel", "arbitrary"),
        name="matmul_residual",
    )(x, a, w)


def _swiglu_kernel(h_ref, wg_ref, wu_ref, o_ref, *, prec):
    h = h_ref[...]
    a = _dot(h, wg_ref[...], prec)
    b = _dot(h, wu_ref[...], prec)
    o_ref[...] = (_silu(a) * b).astype(o_ref.dtype)


def swiglu_up(h, wg, wu, tm, tn):
    m, k = h.shape
    n = wg.shape[1]
    return pl.pallas_call(
        functools.partial(_swiglu_kernel, prec=_prec(wg)),
        out_shape=jax.ShapeDtypeStruct((m, n), wg.dtype),
        grid=(m // tm, n // tn),
        in_specs=[pl.BlockSpec((tm, k), lambda i, j: (i, 0)),
                  pl.BlockSpec((k, tn), lambda i, j: (0, j)),
                  pl.BlockSpec((k, tn), lambda i, j: (0, j))],
        out_specs=pl.BlockSpec((tm, tn), lambda i, j: (i, j)),
        compiler_params=_cp("parallel", "arbitrary"),
        name="swiglu_up",
    )(h, wg, wu)


def _band_scan(q_ref, k_ref, v_ref, dil, slope, sink, emit):
    seq = q_ref.shape[1]
    span = BAND * dil
    qi = lax.broadcasted_iota(jnp.int32, (BAND, 2 * BAND), 0)
    ki = lax.broadcasted_iota(jnp.int32, (BAND, 2 * BAND), 1)
    dist = qi + BAND - ki
    band = (dist >= 0) & (dist <= BAND)
    distf = dist.astype(F32) * float(dil)
    scale = HEAD_DIM ** -0.5

    def block(it, carry):
        r = it % dil
        b = it // dil
        cur = pl.ds(b * span + r, BAND, stride=dil)
        prv = pl.ds(jnp.maximum(b - 1, 0) * span + r, BAND, stride=dil)
        k = jnp.concatenate([k_ref[0, prv, :], k_ref[0, cur, :]], axis=0).astype(BF16)
        v = jnp.concatenate([v_ref[0, prv, :], v_ref[0, cur, :]], axis=0).astype(BF16)
        mask = band & ((ki >= BAND) | (b > 0))
        q = (q_ref[0, cur, :] * scale).astype(BF16)
        s = jnp.where(mask, _dot_nt(q, k) - slope * distf, NEG)
        m = jnp.max(s, axis=-1, keepdims=True)
        if sink is not None:
            m = jnp.maximum(m, sink)
        e = jnp.exp(s - m)
        den = jnp.sum(e, axis=-1, keepdims=True)
        if sink is not None:
            den = den + jnp.exp(sink - m)
        emit(cur, _dot(e.astype(BF16), v) / den, m + jnp.log(den))
        return carry

    lax.fori_loop(0, seq // BAND, block, 0, unroll=16)


def _window_kernel(slope_ref, sink_ref, q_ref, k_ref, v_ref, o_ref):
    h = pl.program_id(1)

    def emit(rows, o, lse):
        o_ref[0, rows, :] = o.astype(o_ref.dtype)

    _band_scan(q_ref, k_ref, v_ref, 1, slope_ref[h], sink_ref[h], emit)


def _dilated_kernel(slope_ref, *refs):
    q_refs, k_refs, v_refs = refs[0:3], refs[3:6], refs[6:9]
    o_ref, o_scr, l_scr = refs[9:]
    h = pl.program_id(1)
    seq = o_ref.shape[1]
    for gi, (_, dil) in enumerate(B_GROUPS):
        def emit(rows, o, lse, gi=gi):
            o_scr[gi, rows, :] = o
            l_scr[gi, rows, :] = jnp.broadcast_to(lse, (BAND, HEAD_DIM))

        _band_scan(q_refs[gi], k_refs[gi], v_refs[gi], dil, slope_ref[gi * B_HPG + h], None, emit)

    def merge(c, carry):
        rows = pl.ds(pl.multiple_of(c * 256, 256), 256)
        la, lb, lc = l_scr[0, rows, :], l_scr[1, rows, :], l_scr[2, rows, :]
        m = jnp.maximum(jnp.maximum(la, lb), lc)
        wa, wb, wc = jnp.exp(la - m), jnp.exp(lb - m), jnp.exp(lc - m)
        out = (wa * o_scr[0, rows, :] + wb * o_scr[1, rows, :] + wc * o_scr[2, rows, :]) / (wa + wb + wc)
        o_ref[0, rows, :] = out.astype(o_ref.dtype)
        return carry

    lax.fori_loop(0, seq // 256, merge, 0)


def window_attention(p3, slopes, sink):
    n_seq, seq, _ = p3.shape
    smem = pl.BlockSpec(memory_space=pltpu.SMEM)
    o = pl.pallas_call(
        _window_kernel,
        out_shape=jax.ShapeDtypeStruct((n_seq, seq, 1024), BF16),
        grid=(n_seq, A_HEADS),
        in_specs=[smem, smem, pl.BlockSpec((1, seq, HEAD_DIM), lambda n, h: (n, 0, h)),
                  pl.BlockSpec((1, seq, HEAD_DIM), lambda n, h: (n, 0, 32 + h // 4)),
                  pl.BlockSpec((1, seq, HEAD_DIM), lambda n, h: (n, 0, 34 + h // 4))],
        out_specs=pl.BlockSpec((1, seq, HEAD_DIM), lambda n, h: (n, 0, h)),
        compiler_params=_cp("parallel", "arbitrary"),
        name="window_attention",
    )(jnp.asarray(slopes, F32), sink, p3, p3, p3)
    return o.reshape(n_seq * seq, 1024)


def dilated_attention(p3, slopes):
    n_seq, seq, _ = p3.shape
    col = lambda f: pl.BlockSpec((1, seq, HEAD_DIM), f)
    q_specs = [col(lambda n, h, gi=gi: (n, 0, 8 * (1 + gi) + h)) for gi in range(N_BGROUPS)]
    k_specs = [col(lambda n, h, gi=gi: (n, 0, 2 * (18 + gi) + h // 4)) for gi in range(N_BGROUPS)]
    v_specs = [col(lambda n, h, gi=gi: (n, 0, 2 * (21 + gi) + h // 4)) for gi in range(N_BGROUPS)]
    o = pl.pallas_call(
        _dilated_kernel,
        out_shape=jax.ShapeDtypeStruct((n_seq, seq, 1024), BF16),
        grid=(n_seq, B_HPG),
        in_specs=[pl.BlockSpec(memory_space=pltpu.SMEM)] + q_specs + k_specs + v_specs,
        out_specs=pl.BlockSpec((1, seq, HEAD_DIM), lambda n, h: (n, 0, h)),
        scratch_shapes=[pltpu.VMEM((N_BGROUPS, seq, HEAD_DIM), F32), pltpu.VMEM((N_BGROUPS, seq, HEAD_DIM), F32)],
        compiler_params=_cp("parallel", "arbitrary"),
        name="dilated_attention",
    )(jnp.asarray(slopes, F32).reshape(-1), *([p3] * 9))
    return o.reshape(n_seq * seq, 1024)


def _decode_kernel(*refs, slopes, dil, win, has_sink):
    if has_sink:
        sink_ref, refs = refs[0], refs[1:]
    q_ref, bk_ref, bv_ref, nk_ref, nv_ref, o_ref, lse_ref = refs
    rows = 4 * DEC_SEQ
    keep = min(dil, DEC_SEQ)
    held = win // dil * keep
    ri = lax.broadcasted_iota(jnp.int32, (rows, held), 0)
    ci = lax.broadcasted_iota(jnp.int32, (rows, held), 1)
    wpos = (ci // keep) * dil + ci % keep
    t1 = win + (ri & 3) - wpos
    valid1 = (wpos >= (ri & 3)) & ((t1 & (dil - 1)) == 0)
    ri2 = lax.broadcasted_iota(jnp.int32, (rows, DEC_PAD), 0)
    ci2 = lax.broadcasted_iota(jnp.int32, (rows, DEC_PAD), 1)
    t2 = (ri2 & 3) - ci2
    valid2 = (t2 >= 0) & ((t2 & (dil - 1)) == 0) & (ci2 < DEC_SEQ)
    hr = lax.broadcasted_iota(jnp.int32, (rows, 1), 0) >> 2
    scale = HEAD_DIM ** -0.5
    for g in range(2):
        lo = g * HEAD_DIM
        slope = jnp.zeros((rows, 1), F32)
        for r in range(4):
            slope = jnp.where(hr == r, float(slopes[g * 4 + r]), slope)
        q = q_ref[0, g] * scale
        s1 = _dot_nt(q, bk_ref[0, :, lo:lo + HEAD_DIM], HI)
        s2 = _dot_nt(q, nk_ref[0, :, lo:lo + HEAD_DIM], HI)
        s1 = jnp.where(valid1, s1 - slope * t1.astype(F32), NEG)
        s2 = jnp.where(valid2, s2 - slope * t2.astype(F32), NEG)
        m = jnp.maximum(jnp.max(s1, axis=-1, keepdims=True), jnp.max(s2, axis=-1, keepdims=True))
        if has_sink:
            sink = jnp.zeros((rows, 1), F32)
            for r in range(4):
                sink = jnp.where(hr == r, sink_ref[g * 4 + r], sink)
            m = jnp.maximum(m, sink)
        e1 = jnp.exp(s1 - m)
        e2 = jnp.exp(s2 - m)
        den = jnp.sum(e1, axis=-1, keepdims=True) + jnp.sum(e2, axis=-1, keepdims=True)
        if has_sink:
            den = den + jnp.exp(sink - m)
        o = _dot(e1, bv_ref[0, :, lo:lo + HEAD_DIM], HI) + _dot(e2, nv_ref[0, :, lo:lo + HEAD_DIM], HI)
        o_ref[0, g] = o / den
        lse_ref[0, g] = jnp.broadcast_to(m + jnp.log(den), (rows, HEAD_DIM))


def window_decode(q, buf_k, buf_v, layer, win, new_k, new_v, slopes, dil, sink):
    nb, held = q.shape[0], buf_k.shape[1]
    has_sink = sink is not None
    in_specs = [
        pl.BlockSpec((1, 2, 16, HEAD_DIM), lambda b: (b, 0, 0, 0)),
        pl.BlockSpec((1, held, 256), lambda b: (layer * nb + b, 0, 0)),
        pl.BlockSpec((1, held, 256), lambda b: (layer * nb + b, 0, 0)),
        pl.BlockSpec((1, DEC_PAD, 256), lambda b: (b, 0, 0)),
        pl.BlockSpec((1, DEC_PAD, 256), lambda b: (b, 0, 0)),
    ]
    ins = [q, buf_k, buf_v, new_k, new_v]
    if has_sink:
        in_specs = [pl.BlockSpec(memory_space=pltpu.SMEM)] + in_specs
        ins = [sink] + ins
    o_shape = jax.ShapeDtypeStruct((nb, 2, 16, HEAD_DIM), F32)
    o_spec = pl.BlockSpec((1, 2, 16, HEAD_DIM), lambda b: (b, 0, 0, 0))
    return pl.pallas_call(
        functools.partial(_decode_kernel, slopes=tuple(float(s) for s in slopes), dil=dil, win=win,
                          has_sink=has_sink),
        out_shape=[o_shape, o_shape],
        grid=(nb,),
        in_specs=in_specs,
        out_specs=[o_spec, o_spec],
        compiler_params=_cp("parallel"),
        name="window_decode",
    )(*ins)


def _combine_kernel(o1, o2, o3, l1, l2, l3, out_ref):
    a, b, c = l1[...], l2[...], l3[...]
    m = jnp.maximum(jnp.maximum(a, b), c)
    wa, wb, wc = jnp.exp(a - m), jnp.exp(b - m), jnp.exp(c - m)
    tot = wa + wb + wc
    out_ref[...] = ((wa * o1[...] + wb * o2[...] + wc * o3[...]) / tot).astype(out_ref.dtype)


def combine_groups(outs, lses, tm, out_dtype):
    m, d = outs[0].shape
    spec = pl.BlockSpec((tm, d), lambda i: (i, 0))
    return pl.pallas_call(
        _combine_kernel,
        out_shape=jax.ShapeDtypeStruct((m, d), out_dtype),
        grid=(m // tm,),
        in_specs=[spec] * 6,
        out_specs=spec,
        compiler_params=_cp("parallel"),
        name="combine_groups",
    )(*outs, *lses)


def _delta_prep_kernel(x_ref, xp_ref, pc_ref, cw_ref, alog_ref, dtb_ref, eb_ref, eg_ref,
                       q_ref, k_ref, v_ref, bb_ref, gb_ref, *, tm, tiles_per_seq, n_valid):
    i = pl.program_id(0)
    x = x_ref[...]
    prev = xp_ref[...]
    if tiles_per_seq is not None:
        prev = jnp.where(i % tiles_per_seq == 0, 0.0, prev)
    row8 = lax.broadcasted_iota(jnp.int32, (8, 1), 0)
    conv = x * cw_ref[CONV_K - 1:CONV_K, :]
    for back in range(1, CONV_K):
        xr = pltpu.roll(x, back, 0)
        head = jnp.where(row8 < back, pltpu.roll(prev, back, 0), xr[:8])
        sh = head if tm == 8 else jnp.concatenate([head, xr[8:]], axis=0)
        conv = conv + sh * cw_ref[CONV_K - 1 - back:CONV_K - back, :]
    act = _silu(conv)
    pc = pc_ref[...]
    beta = _sigmoid(pc)
    gl = -jnp.exp(alog_ref[...]) * _softplus(pc + dtb_ref[...])
    bb = _dot(beta, eb_ref[...], HI)
    gb = _dot(gl, eg_ref[...], HI)
    if n_valid < tm:
        live = lax.broadcasted_iota(jnp.int32, (tm, 1), 0) < n_valid
        act = jnp.where(live, act, 0.0)
        bb = jnp.where(live, bb, 0.0)
        gb = jnp.where(live, gb, 0.0)
    bb_ref[...] = bb
    gb_ref[...] = gb
    for h in range(C_HEADS):
        lo = h * HEAD_DIM
        qh = act[:, lo:lo + HEAD_DIM]
        kh = act[:, C_WIDTH + lo:C_WIDTH + lo + HEAD_DIM]
        qn = qh * lax.rsqrt(jnp.sum(qh * qh, axis=-1, keepdims=True) + EPS)
        q_ref[:, lo:lo + HEAD_DIM] = qn * (HEAD_DIM ** -0.5)
        k_ref[:, lo:lo + HEAD_DIM] = kh * lax.rsqrt(jnp.sum(kh * kh, axis=-1, keepdims=True) + EPS)
    v_ref[...] = act[:, 2 * C_WIDTH:]


def delta_prep(x_arr, x_colblk, xp_arr, xp_colblk, xp_rowmap, pc, conv_w, a_log, dt_bias, tm, tiles_per_seq, n_valid):
    m = pc.shape[0]
    w3 = 3 * C_WIDTH
    alog = jnp.zeros((1, 128), F32).at[0, 8:16].set(a_log)
    dtb = jnp.zeros((1, 128), F32).at[0, 8:16].set(dt_bias)
    lane_head = np.arange(C_WIDTH) // HEAD_DIM
    eb = jnp.asarray((np.arange(128)[:, None] == lane_head[None, :]).astype(np.float32))
    eg = jnp.asarray((np.arange(128)[:, None] == lane_head[None, :] + 8).astype(np.float32))
    full = lambda shape: pl.BlockSpec(shape, lambda i: (0, 0))
    o_shape = jax.ShapeDtypeStruct((m, C_WIDTH), F32)
    o_spec = pl.BlockSpec((tm, C_WIDTH), lambda i: (i, 0))
    return pl.pallas_call(
        functools.partial(_delta_prep_kernel, tm=tm, tiles_per_seq=tiles_per_seq, n_valid=n_valid),
        out_shape=[o_shape] * 5,
        grid=(m // tm,),
        in_specs=[pl.BlockSpec((tm, w3), lambda i: (i, x_colblk)),
                  pl.BlockSpec((8, w3), lambda i: (xp_rowmap(i), xp_colblk)),
                  pl.BlockSpec((tm, 128), lambda i: (i, 0)),
                  full((CONV_K, w3)), full((1, 128)), full((1, 128)), full((128, C_WIDTH)), full((128, C_WIDTH))],
        out_specs=[o_spec] * 5,
        compiler_params=_cp("parallel"),
        name="delta_prep",
    )(x_arr, xp_arr, pc, conv_w, alog, dtb, eb, eg)


def _dot_hi(a, b):
    return _dot(a, b, HI)


def _unit_lower_inverse(low, c, mm=_dot_hi):
    ii = lax.broadcasted_iota(jnp.int32, (c, c), 0)
    jj = lax.broadcasted_iota(jnp.int32, (c, c), 1)
    eye = (ii == jj).astype(F32)
    ld = jnp.where((ii >> 3) == (jj >> 3), low, 0.0)
    l2 = mm(ld, ld)
    l4 = mm(l2, l2)
    inv = mm(mm(eye - ld, eye + l2), eye + l4)
    s = 3
    while (1 << s) < c:
        off = ((ii >> (s + 1)) == (jj >> (s + 1))) & ((ii >> s) != (jj >> s))
        lo = jnp.where(off, low, 0.0)
        inv = inv - mm(mm(inv, lo), inv)
        s += 1
    return inv


def _split_bf16(x, n):
    parts, rest = [], x
    for _ in range(n):
        part = rest.astype(BF16)
        parts.append(part)
        rest = rest - part.astype(F32)
    return parts


def _dot_exact_lhs(a, b, nt=False):
    mm = _dot_nt if nt else _dot
    hi, mid, lo = _split_bf16(b, 3)
    return mm(a, hi) + (mm(a, mid) + mm(a, lo))


def _bdot(a, b):
    return lax.dot_general(a, b, (((2,), (1,)), ((0,), (0,))), preferred_element_type=F32)


def _bdot_hi(a, b):
    return lax.dot_general(a, b, (((2,), (1,)), ((0,), (0,))), precision=HI, preferred_element_type=F32)


def _bdot_nt(a, b):
    return lax.dot_general(a, b, (((2,), (2,)), ((0,), (0,))), preferred_element_type=F32)


def _bdot3(a, b):
    a_hi, a_lo = _split_bf16(a, 2)
    b_hi, b_lo = _split_bf16(b, 2)
    return _bdot(a_hi, b_hi) + (_bdot(a_hi, b_lo) + _bdot(a_lo, b_hi))


def _delta_wy_kernel(q_ref, k_ref, v_ref, bb_ref, gb_ref, u_ref, w_ref, at_ref, qg_ref, kd_ref, egl_ref, *, nch):
    c = CHUNK
    ii = lax.broadcasted_iota(jnp.int32, (c, c), 0)
    jj = lax.broadcasted_iota(jnp.int32, (c, c), 1)
    tri = (ii >= jj).astype(BF16)
    sel0 = (lax.broadcasted_iota(jnp.int32, (c, HEAD_DIM), 1) == 0).astype(BF16)
    heads = [slice(h * HEAD_DIM, (h + 1) * HEAD_DIM) for h in range(C_HEADS)]
    for t in range(nch):
        rows = slice(t * c, (t + 1) * c)
        q, k, v, bb = q_ref[rows, :], k_ref[rows, :], v_ref[rows, :], bb_ref[rows, :]
        gcb = _dot_exact_lhs(tri, gb_ref[rows, :])
        eg = jnp.exp(gcb)
        g_last = gcb[c - 1:c, :]
        kd_ref[rows, :] = (k * jnp.exp(g_last - gcb)).astype(BF16)
        qg_ref[rows, :] = (q * eg).astype(BF16)
        egl_ref[t] = jnp.broadcast_to(jnp.exp(g_last), (8, C_WIDTH))
        kb = k * bb
        vb = v * bb
        kbe = kb * eg
        gcr = jnp.stack([_dot_exact_lhs(sel0, gcb[:, hs], nt=True) for hs in heads])
        gci = jnp.stack([gcb[:, hs][:, :c] for hs in heads])
        dec = jnp.where(ii >= jj, jnp.exp(jnp.minimum(gci - gcr, 0.0)), 0.0)
        lhs = jnp.stack([jnp.concatenate([kb[:, hs], q[:, hs]], axis=0) for hs in heads]).astype(BF16)
        qk = _bdot_nt(lhs, jnp.stack([k[:, hs] for hs in heads]).astype(BF16))
        low = jnp.where(ii > jj, qk[:, :c] * dec, 0.0)
        attn = qk[:, c:] * dec
        tmat = _unit_lower_inverse(low, c, _bdot3)
        rhs = jnp.stack([jnp.concatenate([vb[:, hs], kbe[:, hs]], axis=1) for hs in heads]).astype(BF16)
        uw = _bdot(tmat.astype(BF16), rhs)
        for h, hs in enumerate(heads):
            u_ref[rows, hs] = uw[h, :, :HEAD_DIM]
            w_ref[rows, hs] = uw[h, :, HEAD_DIM:].astype(BF16)
            at_ref[rows, h * c:(h + 1) * c] = attn[h].astype(BF16)


def delta_wy(q, k, v, bb, gb, nch):
    m = q.shape[0]
    rows = nch * CHUNK
    row = pl.BlockSpec((rows, C_WIDTH), lambda i: (i, 0))
    sds = lambda w, dt: jax.ShapeDtypeStruct((m, w), dt)
    return pl.pallas_call(
        functools.partial(_delta_wy_kernel, nch=nch),
        out_shape=[sds(C_WIDTH, F32), sds(C_WIDTH, BF16), sds(C_HEADS * CHUNK, BF16), sds(C_WIDTH, BF16),
                   sds(C_WIDTH, BF16), jax.ShapeDtypeStruct((m // CHUNK, 8, C_WIDTH), F32)],
        grid=(m // rows,),
        in_specs=[row] * 5,
        out_specs=[row, row, pl.BlockSpec((rows, C_HEADS * CHUNK), lambda i: (i, 0)), row, row,
                   pl.BlockSpec((nch, 8, C_WIDTH), lambda i: (i, 0, 0))],
        compiler_params=_cp("parallel"),
        name="delta_wy",
    )(q, k, v, bb, gb)


def _delta_rec_kernel(u_ref, w_ref, at_ref, qg_ref, kd_ref, egl_ref, z_ref, s0_ref, nw_ref, o_ref, sout_ref, s_ref,
                      *, n_seq):
    ci = pl.program_id(0)
    c = CHUNK

    @pl.when(ci == 0)
    def _():
        s_ref[...] = s0_ref[...]

    nw = nw_ref[...]
    for n in range(n_seq):
        idx = [(n * C_HEADS + h, slice(h * HEAD_DIM, (h + 1) * HEAD_DIM)) for h in range(C_HEADS)]
        st = [s_ref[i] for i, _ in idx]
        r = [_dot(jnp.concatenate([w_ref[n, :, hs], qg_ref[n, :, hs]], axis=0), s.astype(BF16))
             for (_, hs), s in zip(idx, st)]
        vn = [(u_ref[n, :, hs] - rr[:c]).astype(BF16) for (_, hs), rr in zip(idx, r)]
        for h, ((i, hs), s, rr, vb) in enumerate(zip(idx, st, r, vn)):
            o = rr[c:] + _dot(at_ref[n, :, h * c:(h + 1) * c], vb)
            s_ref[i] = s * egl_ref[n, 0, 0:1, hs] + _dot_tn(kd_ref[n, :, hs], vb)
            on = o * lax.rsqrt(jnp.mean(o * o, axis=-1, keepdims=True) + EPS) * nw * _silu(z_ref[n, :, hs])
            o_ref[n, :, hs] = on.astype(o_ref.dtype)

    @pl.when(ci == pl.num_programs(0) - 1)
    def _():
        sout_ref[...] = s_ref[...]


def delta_rec(u, w, at, qg, kd, egl, z_arr, z_colblk, s0, norm_w, n_seq, seq):
    nc = seq // CHUNK
    v3 = lambda x: x.reshape(n_seq, seq, x.shape[-1])
    row = pl.BlockSpec((n_seq, CHUNK, C_WIDTH), lambda t: (0, t, 0))
    s_spec = pl.BlockSpec((n_seq * C_HEADS, HEAD_DIM, HEAD_DIM), lambda t: (0, 0, 0))
    o, s_fin = pl.pallas_call(
        functools.partial(_delta_rec_kernel, n_seq=n_seq),
        out_shape=[jax.ShapeDtypeStruct((n_seq, seq, C_WIDTH), BF16),
                   jax.ShapeDtypeStruct((n_seq * C_HEADS, HEAD_DIM, HEAD_DIM), F32)],
        grid=(nc,),
        in_specs=[row, row, pl.BlockSpec((n_seq, CHUNK, C_HEADS * CHUNK), lambda t: (0, t, 0)), row, row,
                  pl.BlockSpec((n_seq, 1, 8, C_WIDTH), lambda t: (0, t, 0, 0)),
                  pl.BlockSpec((n_seq, CHUNK, C_WIDTH), lambda t: (0, t, z_colblk)),
                  s_spec, pl.BlockSpec((1, HEAD_DIM), lambda t: (0, 0))],
        out_specs=[row, s_spec],
        scratch_shapes=[pltpu.VMEM((n_seq * C_HEADS, HEAD_DIM, HEAD_DIM), F32)],
        compiler_params=_cp("arbitrary"),
        name="delta_rec",
    )(v3(u), v3(w), v3(at), v3(qg), v3(kd), egl.reshape(n_seq, nc, 8, C_WIDTH), v3(z_arr),
      s0.reshape(n_seq * C_HEADS, HEAD_DIM, HEAD_DIM), norm_w.reshape(1, HEAD_DIM))
    return o.reshape(n_seq * seq, C_WIDTH), s_fin.reshape(n_seq, C_HEADS, HEAD_DIM, HEAD_DIM)


def _delta_chunk_kernel(q_ref, k_ref, v_ref, bb_ref, gb_ref, z_ref, s0_ref, nw_ref, o_ref, sout_ref, s_ref, *, c):
    ci = pl.program_id(1)

    @pl.when(ci == 0)
    def _():
        s_ref[...] = s0_ref[0]

    ii = lax.broadcasted_iota(jnp.int32, (c, c), 0)
    jj = lax.broadcasted_iota(jnp.int32, (c, c), 1)
    tri = (ii >= jj).astype(F32)
    sel0 = (lax.broadcasted_iota(jnp.int32, (c, HEAD_DIM), 1) == 0).astype(F32)
    nw = nw_ref[...]
    heads = [slice(h * HEAD_DIM, (h + 1) * HEAD_DIM) for h in range(C_HEADS)]
    q = [q_ref[:, hs] for hs in heads]
    k = [k_ref[:, hs] for hs in heads]
    bb = [bb_ref[:, hs] for hs in heads]
    gcb = [_dot(tri, gb_ref[:, hs], HI) for hs in heads]
    gcr = [_dot_nt(sel0, g, HI) for g in gcb]
    dec = [jnp.where(ii >= jj, jnp.exp(jnp.minimum(g[:, :c] - r, 0.0)), 0.0) for g, r in zip(gcb, gcr)]
    kb = [kh * bh for kh, bh in zip(k, bb)]
    low = jnp.stack([jnp.where(ii > jj, _dot_nt(kbh, kh, HI) * d, 0.0) for kbh, kh, d in zip(kb, k, dec)])
    tmat = _unit_lower_inverse(low, c, _bdot_hi)
    eg = [jnp.exp(g) for g in gcb]
    u = [_dot(tmat[h], v_ref[:, hs] * bb[h], HI) for h, hs in enumerate(heads)]
    w = [_dot(tmat[h], kb[h] * eg[h], HI) for h in range(C_HEADS)]
    s = [s_ref[h] for h in range(C_HEADS)]
    v_new = [uh - _dot(wh, sh, HI) for uh, wh, sh in zip(u, w, s)]
    attn = [_dot_nt(qh, kh, HI) * d for qh, kh, d in zip(q, k, dec)]
    o = [_dot(qh * e, sh, HI) + _dot(a, vn, HI) for qh, e, sh, a, vn in zip(q, eg, s, attn, v_new)]
    for h, hs in enumerate(heads):
        g_last = gcb[h][c - 1:c, :]
        s_ref[h] = s[h] * jnp.exp(g_last) + _dot_tn(k[h] * jnp.exp(g_last - gcb[h]), v_new[h], HI)
        on = o[h] * lax.rsqrt(jnp.mean(o[h] * o[h], axis=-1, keepdims=True) + EPS) * nw * _silu(z_ref[:, hs])
        o_ref[:, hs] = on.astype(o_ref.dtype)

    @pl.when(ci == pl.num_programs(1) - 1)
    def _():
        sout_ref[0] = s_ref[...]


def delta_chunks(q, k, v, bb, gb, z_arr, z_colblk, s0, s0_off, norm_w, n_seq, c, out_dtype):
    m = q.shape[0]
    nc = m // n_seq // c
    row = pl.BlockSpec((c, C_WIDTH), lambda n, t: (n * nc + t, 0))
    s_spec = pl.BlockSpec((1, C_HEADS, HEAD_DIM, HEAD_DIM), lambda n, t: (n, 0, 0, 0))
    s0_spec = pl.BlockSpec((1, C_HEADS, HEAD_DIM, HEAD_DIM), lambda n, t: (s0_off + n, 0, 0, 0))
    return pl.pallas_call(
        functools.partial(_delta_chunk_kernel, c=c),
        out_shape=[jax.ShapeDtypeStruct((m, C_WIDTH), out_dtype),
                   jax.ShapeDtypeStruct((n_seq, C_HEADS, HEAD_DIM, HEAD_DIM), F32)],
        grid=(n_seq, nc),
        in_specs=[row, row, row, row, row,
                  pl.BlockSpec((c, C_WIDTH), lambda n, t: (n * nc + t, z_colblk)),
                  s0_spec, pl.BlockSpec((1, HEAD_DIM), lambda n, t: (0, 0))],
        out_specs=[row, s_spec],
        scratch_shapes=[pltpu.VMEM((C_HEADS, HEAD_DIM, HEAD_DIM), F32)],
        compiler_params=_cp("parallel", "arbitrary"),
        name="delta_chunks",
    )(q, k, v, bb, gb, z_arr, s0, norm_w.reshape(1, HEAD_DIM))


def _merge_kernel(oa_ref, ob_ref, oc_ref, ga_ref, gb_ref, gc_ref, wa_ref, wb_ref, wc_ref, o_ref, *, prec):
    ya = _dot(oa_ref[...], wa_ref[...], prec)
    yb = _dot(ob_ref[...], wb_ref[...], prec)
    yc = _dot(oc_ref[...], wc_ref[...], prec)
    t = _sigmoid(ga_ref[...]) * ya + _sigmoid(gb_ref[...]) * yb + _sigmoid(gc_ref[...]) * yc
    o_ref[...] = t.astype(o_ref.dtype)


def merge_branches(oa, ob, oc, p, w_a, w_b, w_c, tm, tn):
    m = oa.shape[0]
    gate0 = 10240 // tn
    per = D_MODEL // tn
    o_spec = pl.BlockSpec((tm, 1024), lambda i, j: (i, 0))
    w_spec = pl.BlockSpec((1024, tn), lambda i, j: (0, j))
    gate_spec = lambda b: pl.BlockSpec((tm, tn), lambda i, j: (i, gate0 + b * per + j))
    return pl.pallas_call(
        functools.partial(_merge_kernel, prec=_prec(w_a)),
        out_shape=jax.ShapeDtypeStruct((m, D_MODEL), w_a.dtype),
        grid=(m // tm, D_MODEL // tn),
        in_specs=[o_spec, o_spec, o_spec, gate_spec(0), gate_spec(1), gate_spec(2), w_spec, w_spec, w_spec],
        out_specs=pl.BlockSpec((tm, tn), lambda i, j: (i, j)),
        compiler_params=_cp("parallel", "arbitrary"),
        name="merge_branches",
    )(oa, ob, oc, p, p, p, w_a, w_b, w_c)


def _router_kernel(x_ref, g_ref, r_ref, h_ref, sel_ref):
    x = x_ref[...]
    hn = x * lax.rsqrt(jnp.mean(x * x, axis=-1, keepdims=True) + EPS) * g_ref[...]
    h_ref[...] = hn
    logits = _dot(hn, r_ref[...], HI)
    lane_i = lax.broadcasted_iota(jnp.int32, logits.shape, 1)
    lane = lane_i.astype(F32)
    logits = jnp.where(lane_i < N_EXPERTS, logits, NEG)
    v1 = jnp.max(logits, axis=-1, keepdims=True)
    i1 = jnp.min(jnp.where(logits == v1, lane, 128.0), axis=-1, keepdims=True)
    rest = jnp.where(lane == i1, NEG, logits)
    v2 = jnp.max(rest, axis=-1, keepdims=True)
    i2 = jnp.min(jnp.where(rest == v2, lane, 128.0), axis=-1, keepdims=True)
    e = jnp.exp(v2 - v1)
    tot = 1.0 + e
    sel_ref[...] = jnp.where(lane_i == 0, i1, jnp.where(lane_i == 1, i2, jnp.where(lane_i == 2, 1.0 / tot, e / tot)))


def router(x, g, router_w, tm):
    m, d = x.shape
    rw = jnp.zeros((d, 128), F32).at[:, :N_EXPERTS].set(router_w)
    return pl.pallas_call(
        _router_kernel,
        out_shape=[jax.ShapeDtypeStruct((m, d), F32), jax.ShapeDtypeStruct((m, 128), F32)],
        grid=(m // tm,),
        in_specs=[pl.BlockSpec((tm, d), lambda i: (i, 0)), pl.BlockSpec((1, d), lambda i: (0, 0)),
                  pl.BlockSpec((d, 128), lambda i: (0, 0))],
        out_specs=[pl.BlockSpec((tm, d), lambda i: (i, 0)), pl.BlockSpec((tm, 128), lambda i: (i, 0))],
        compiler_params=_cp("parallel"),
        name="router",
    )(x, g.reshape(1, d), rw)


MOE_TM = 256


def _moe_plan(sel, m):
    i1 = sel[:, 0].astype(jnp.int32)
    i2 = sel[:, 1].astype(jnp.int32)
    e_flat = jnp.concatenate([i1, i2])
    tok = jnp.tile(jnp.arange(m, dtype=jnp.int32), 2)
    onehot = (e_flat[:, None] == jnp.arange(N_EXPERTS, dtype=jnp.int32)[None, :]).astype(jnp.int32)
    counts = jnp.sum(onehot, axis=0)
    rank = jnp.take_along_axis(jnp.cumsum(onehot, axis=0) - onehot, e_flat[:, None], axis=1)[:, 0]
    padded = (counts + MOE_TM - 1) // MOE_TM * MOE_TM
    ends = jnp.cumsum(padded)
    pos = (ends - padded)[e_flat] + rank
    n_tiles = (2 * m) // MOE_TM + N_EXPERTS
    row_token = jnp.zeros((n_tiles * MOE_TM,), jnp.int32).at[pos].set(tok)
    tiles = jnp.arange(n_tiles, dtype=jnp.int32)
    n_used = ends[-1] // MOE_TM
    tile_src = jnp.minimum(tiles, n_used - 1)
    tile_expert = jnp.sum((tile_src * MOE_TM)[:, None] >= ends[None, :], axis=1).astype(jnp.int32)
    tile_used = (tiles < n_used).astype(jnp.int32)
    return row_token, pos[:m], pos[m:], tile_expert, tile_src, tile_used


def _moe_gather_kernel(tok_ref, used_ref, h_hbm, o_ref, buf, sem):
    i = pl.program_id(0)

    @pl.when(used_ref[i] == 1)
    def _():
        def issue(r, carry):
            pltpu.make_async_copy(h_hbm.at[pl.ds(tok_ref[i * MOE_TM + r], 1)], buf.at[pl.ds(r, 1)], sem).start()
            return carry
        lax.fori_loop(0, MOE_TM, issue, 0)
        pltpu.make_async_copy(h_hbm.at[pl.ds(0, MOE_TM)], buf, sem).wait()
        o_ref[...] = buf[...].astype(o_ref.dtype)

    @pl.when(used_ref[i] == 0)
    def _():
        o_ref[...] = jnp.zeros_like(o_ref)


def moe_gather(h, row_token, tile_used):
    d = h.shape[1]
    n_tiles = tile_used.shape[0]
    return pl.pallas_call(
        _moe_gather_kernel,
        out_shape=jax.ShapeDtypeStruct((n_tiles * MOE_TM, d), BF16),
        grid_spec=pltpu.PrefetchScalarGridSpec(
            num_scalar_prefetch=2,
            grid=(n_tiles,),
            in_specs=[pl.BlockSpec(memory_space=pl.ANY)],
            out_specs=pl.BlockSpec((MOE_TM, d), lambda i, tok, used: (i, 0)),
            scratch_shapes=[pltpu.VMEM((MOE_TM, d), F32), pltpu.SemaphoreType.DMA(())]),
        compiler_params=_cp("arbitrary"),
        name="moe_gather",
    )(row_token, tile_used, h)


def _moe_up_kernel(te_ref, ts_ref, tu_ref, h_ref, wg_ref, wu_ref, o_ref, wg_bf, wu_bf):
    t = pl.program_id(1)
    used = tu_ref[t] == 1
    new_block = jnp.logical_or(t == 0, te_ref[t] != te_ref[jnp.maximum(t - 1, 0)])

    @pl.when(jnp.logical_and(used, new_block))
    def _():
        wg_bf[...] = wg_ref[0].astype(BF16)
        wu_bf[...] = wu_ref[0].astype(BF16)

    @pl.when(used)
    def _():
        h = h_ref[...]
        o_ref[...] = (_silu(_dot(h, wg_bf[...])) * _dot(h, wu_bf[...])).astype(o_ref.dtype)

    @pl.when(jnp.logical_not(used))
    def _():
        o_ref[...] = jnp.zeros_like(o_ref)


def moe_up(hs, wg, wu, plan, tn):
    tile_expert, tile_src, tile_used = plan
    d, f = wg.shape[1], wg.shape[2]
    n_tiles = tile_used.shape[0]
    w_spec = pl.BlockSpec((1, d, tn), lambda j, t, te, ts, tu: (te[t], 0, j))
    return pl.pallas_call(
        _moe_up_kernel,
        out_shape=jax.ShapeDtypeStruct((n_tiles * MOE_TM, f), BF16),
        grid_spec=pltpu.PrefetchScalarGridSpec(
            num_scalar_prefetch=3,
            grid=(f // tn, n_tiles),
            in_specs=[pl.BlockSpec((MOE_TM, d), lambda j, t, te, ts, tu: (ts[t], 0)), w_spec, w_spec],
            out_specs=pl.BlockSpec((MOE_TM, tn), lambda j, t, te, ts, tu: (t, j)),
            scratch_shapes=[pltpu.VMEM((d, tn), BF16), pltpu.VMEM((d, tn), BF16)]),
        compiler_params=_cp("arbitrary", "arbitrary"),
        name="moe_up",
    )(tile_expert, tile_src, tile_used, hs, wg, wu)


def _moe_down_kernel(te_ref, ts_ref, tu_ref, a_ref, w_ref, o_ref, w_bf):
    t = pl.program_id(1)
    used = tu_ref[t] == 1
    new_block = jnp.logical_or(t == 0, te_ref[t] != te_ref[jnp.maximum(t - 1, 0)])

    @pl.when(jnp.logical_and(used, new_block))
    def _():
        w_bf[...] = w_ref[0].astype(BF16)

    @pl.when(used)
    def _():
        o_ref[...] = _dot(a_ref[...], w_bf[...])

    @pl.when(jnp.logical_not(used))
    def _():
        o_ref[...] = jnp.zeros_like(o_ref)


def moe_down(act, wd, plan, tn):
    tile_expert, tile_src, tile_used = plan
    f, d = wd.shape[1], wd.shape[2]
    n_tiles = tile_used.shape[0]
    return pl.pallas_call(
        _moe_down_kernel,
        out_shape=jax.ShapeDtypeStruct((n_tiles * MOE_TM, d), F32),
        grid_spec=pltpu.PrefetchScalarGridSpec(
            num_scalar_prefetch=3,
            grid=(d // tn, n_tiles),
            in_specs=[pl.BlockSpec((MOE_TM, f), lambda j, t, te, ts, tu: (ts[t], 0)),
                      pl.BlockSpec((1, f, tn), lambda j, t, te, ts, tu: (te[t], 0, j))],
            out_specs=pl.BlockSpec((MOE_TM, tn), lambda j, t, te, ts, tu: (t, j)),
            scratch_shapes=[pltpu.VMEM((f, tn), BF16)]),
        compiler_params=_cp("arbitrary", "arbitrary"),
        name="moe_down",
    )(tile_expert, tile_src, tile_used, act, wd)


def _moe_combine_kernel(p1_ref, p2_ref, x_ref, sel_ref, y_hbm, o_ref, b1, b2, sems, *, tm):
    base = pl.program_id(0) * tm

    def issue(r, carry):
        pltpu.make_async_copy(y_hbm.at[pl.ds(p1_ref[base + r], 1)], b1.at[pl.ds(r, 1)], sems.at[0]).start()
        pltpu.make_async_copy(y_hbm.at[pl.ds(p2_ref[base + r], 1)], b2.at[pl.ds(r, 1)], sems.at[1]).start()
        return carry
    lax.fori_loop(0, tm, issue, 0)
    pltpu.make_async_copy(y_hbm.at[pl.ds(0, tm)], b1, sems.at[0]).wait()
    pltpu.make_async_copy(y_hbm.at[pl.ds(0, tm)], b2, sems.at[1]).wait()
    o_ref[...] = x_ref[...] + (sel_ref[:, 2:3] * b1[...] + sel_ref[:, 3:4] * b2[...])


def moe_combine(x, sel, y, pos1, pos2, tm):
    m, d = x.shape
    return pl.pallas_call(
        functools.partial(_moe_combine_kernel, tm=tm),
        out_shape=jax.ShapeDtypeStruct((m, d), F32),
        grid_spec=pltpu.PrefetchScalarGridSpec(
            num_scalar_prefetch=2,
            grid=(m // tm,),
            in_specs=[pl.BlockSpec((tm, d), lambda i, p1, p2: (i, 0)),
                      pl.BlockSpec((tm, 128), lambda i, p1, p2: (i, 0)),
                      pl.BlockSpec(memory_space=pl.ANY)],
            out_specs=pl.BlockSpec((tm, d), lambda i, p1, p2: (i, 0)),
            scratch_shapes=[pltpu.VMEM((tm, d), F32), pltpu.VMEM((tm, d), F32), pltpu.SemaphoreType.DMA((2,))]),
        compiler_params=_cp("arbitrary"),
        name="moe_combine",
    )(pos1, pos2, x, sel, y)


def moe_ffn(x, g, router_w, wg, wu, wd):
    m = x.shape[0]
    h, sel = router(x, g, router_w, 320)
    row_token, pos1, pos2, tile_expert, tile_src, tile_used = _moe_plan(sel, m)
    plan = (tile_expert, tile_src, tile_used)
    hs = moe_gather(h, row_token, tile_used)
    act = moe_up(hs, wg, wu, plan, 1024)
    y = moe_down(act, wd, plan, 512)
    return moe_combine(x, sel, y, pos1, pos2, 128)


def _split_w_in(w):
    qa, ka, va, qb, kb, vb, cqkv, z, cb, ca, gates = jnp.split(w, np.cumsum(PROJ_SIZES)[:-1].tolist(), axis=-1)
    main = jnp.concatenate([qa, qb, ka, va, kb, vb, cqkv, z, gates], axis=-1)
    small = jnp.concatenate([cb, ca, jnp.zeros((w.shape[0], 112), w.dtype)], axis=-1)
    return main, small


def _mix_prompt(h, lp):
    w_main, w_small, sink, conv_w, a_log, dt_bias, norm_w, w_a, w_b, w_c = lp
    m = h.shape[0]
    p = matmul(h, w_main, 2048, 1024)
    pc = matmul(h, w_small, 512, 128)
    p3 = p.reshape(BATCH, SEQ, PW)
    oa = window_attention(p3, _alibi(A_HEADS), sink)
    ob = dilated_attention(p3, _alibi(N_BGROUPS * B_HPG).reshape(N_BGROUPS, B_HPG))
    tm = 256
    q, k, v, bb, gb = delta_prep(p, 2, p, 2, lambda i: jnp.maximum(i * (tm // 8) - 1, 0), pc, conv_w, a_log,
                                 dt_bias, tm, SEQ // tm, tm)
    s0 = jnp.zeros((BATCH, C_HEADS, HEAD_DIM, HEAD_DIM), F32)
    u, w, at, qg, kd, egl = delta_wy(q, k, v, bb, gb, 2)
    oc, s_fin = delta_rec(u, w, at, qg, kd, egl, p, 9, s0, norm_w, BATCH, SEQ)
    t = merge_branches(oa, ob, oc, p, w_a, w_b, w_c, 1024, 512)
    p4 = p.reshape(BATCH, SEQ, PW)
    kv = lambda blk, win: p4[:, SEQ - win:, blk * 256:(blk + 1) * 256].reshape(BATCH, win, 2, HEAD_DIM)
    state = [kv(16, A_WINDOW), kv(17, A_WINDOW)]
    for gi, (win, _) in enumerate(B_GROUPS):
        state += [kv(18 + gi, win), kv(21 + gi, win)]
    state += [p4[:, SEQ - (CONV_K - 1):, 6144:9216], s_fin]
    return t, state


def _mix_sample(h, lp, caches, layer):
    w_main, w_small, sink, conv_w, a_log, dt_bias, norm_w, w_a, w_b, w_c = lp
    c_ak, c_av, b1k, b1v, b2k, b2v, b3k, b3v, conv_all, s0_all = caches
    nb, nt = DEC_BATCH, DEC_SEQ
    m = nb * nt
    p = matmul(h, w_main, m, 512)
    pc = matmul(h, w_small, m, 128)
    p3 = p.reshape(nb, nt, PW)

    def heads_q(blk):
        qq = p3[:, :, blk * 1024:(blk + 1) * 1024].reshape(nb, nt, 2, 4, HEAD_DIM)
        return jnp.transpose(qq, (0, 2, 3, 1, 4)).reshape(nb, 2, 16, HEAD_DIM)

    def heads_o(o):
        oo = o.reshape(nb, 2, 4, nt, HEAD_DIM)
        return jnp.transpose(oo, (0, 3, 1, 2, 4)).reshape(m, 1024)

    def new_rows(blk):
        x = p3[:, :, blk * 256:(blk + 1) * 256]
        return x.reshape(nb, nt, 2, HEAD_DIM), jnp.pad(x, ((0, 0), (0, DEC_PAD - nt), (0, 0)))

    def reach(buf, dil):
        win, keep = buf.shape[2], min(dil, nt)
        x = buf.reshape(DEPTH * nb, win // dil, dil, 2, HEAD_DIM)[:, :, :keep]
        return x.reshape(DEPTH * nb, win // dil * keep, 256), win

    ka, ka_p = new_rows(16)
    va, va_p = new_rows(17)
    oa, _ = window_decode(heads_q(0), reach(c_ak, 1)[0], reach(c_av, 1)[0], layer, A_WINDOW, ka_p, va_p,
                          _alibi(A_HEADS), 1, sink)
    oa = heads_o(oa)
    appended = [ka, va]
    slopes_b = _alibi(N_BGROUPS * B_HPG).reshape(N_BGROUPS, B_HPG)
    outs, lses = [], []
    for gi, ((_, dil), (bk, bv)) in enumerate(zip(B_GROUPS, ((b1k, b1v), (b2k, b2v), (b3k, b3v)))):
        kn, kn_p = new_rows(18 + gi)
        vn, vn_p = new_rows(21 + gi)
        (rk, win), (rv, _) = reach(bk, dil), reach(bv, dil)
        o, lse = window_decode(heads_q(1 + gi), rk, rv, layer, win, kn_p, vn_p, slopes_b[gi], dil, None)
        outs.append(heads_o(o))
        lses.append(heads_o(lse))
        appended += [kn, vn]
    ob = combine_groups(outs, lses, m, F32)

    pad_rows = lambda x: jnp.pad(x.reshape(nb, nt, -1), ((0, 0), (0, DEC_PAD - nt), (0, 0))).reshape(nb * DEC_PAD, -1)
    cq = p3[:, :, 6144:9216]
    xs = pad_rows(cq)
    xprev = jnp.pad(conv_all[layer], ((0, 0), (8 - (CONV_K - 1), 0), (0, 0))).reshape(nb * 8, 3 * C_WIDTH)
    q, k, v, bb, gb = delta_prep(xs, 0, xprev, 0, lambda i: i, pad_rows(pc), conv_w, a_log, dt_bias,
                                 DEC_PAD, None, nt)
    zs = pad_rows(p3[:, :, 9216:10240])
    s0_flat = s0_all.reshape(DEPTH * nb, C_HEADS, HEAD_DIM, HEAD_DIM)
    oc, s_fin = delta_chunks(q, k, v, bb, gb, zs, 0, s0_flat, layer * nb, norm_w, nb, DEC_PAD, F32)
    oc = oc.reshape(nb, DEC_PAD, C_WIDTH)[:, :nt].reshape(m, C_WIDTH)
    t = merge_branches(oa, ob, oc, p, w_a, w_b, w_c, m, 512)
    appended += [cq, s_fin]
    return t, appended


def kernel(x_prompt, x_sample, cache_a_k, cache_a_v, cache_b1_k, cache_b1_v, cache_b2_k, cache_b2_v, cache_b3_k, cache_b3_v, state_c_conv, state_c_rec, norm_mix, norm_ffn, norm_final, w_in, attn_sink, conv_w, a_log, dt_bias, norm_delta, w_out_a, w_out_b, w_out_c, w_out, ffn_w_gate, ffn_w_up, ffn_w_down, router_w, moe_w_gate, moe_w_up, moe_w_down):
    mp, ms = BATCH * SEQ, DEC_BATCH * DEC_SEQ
    xp = x_prompt.reshape(mp, D_MODEL)
    xs = x_sample.reshape(ms, D_MODEL)
    tmp = 1024
    p_states, s_states = [], []
    caches = (cache_a_k, cache_a_v, cache_b1_k, cache_b1_v, cache_b2_k, cache_b2_v, cache_b3_k, cache_b3_v,
              state_c_conv, state_c_rec)
    for l in range(DEPTH):
        w_main, w_small = _split_w_in(w_in[l])
        shared = (attn_sink[l], conv_w[l], a_log[l], dt_bias[l], norm_delta[l])
        lp_s = (w_main, w_small, *shared, w_out_a[l], w_out_b[l], w_out_c[l])
        lp_p = (w_main.astype(BF16), w_small.astype(BF16), *shared,
                w_out_a[l].astype(BF16), w_out_b[l].astype(BF16), w_out_c[l].astype(BF16))
        tp, sp = _mix_prompt(rmsnorm(xp, norm_mix[l], BF16, tmp), lp_p)
        ts, ss = _mix_sample(rmsnorm(xs, norm_mix[l], F32, ms), lp_s, caches, l)
        xp = matmul_residual(xp, tp, w_out[l].astype(BF16), tmp, 512)
        xs = matmul_residual(xs, ts, w_out[l], ms, 512)
        p_states.append(sp)
        s_states.append(ss)
        i = l // 2
        if l % 2 == 0:
            wg, wu, wd = ffn_w_gate[i], ffn_w_up[i], ffn_w_down[i]
            hp = rmsnorm(xp, norm_ffn[l], BF16, tmp)
            hs = rmsnorm(xs, norm_ffn[l], F32, ms)
            xp = matmul_residual(xp, swiglu_up(hp, wg.astype(BF16), wu.astype(BF16), tmp, 512), wd.astype(BF16),
                                 tmp, 512)
            xs = matmul_residual(xs, swiglu_up(hs, wg, wu, ms, 512), wd, ms, 512)
        else:
            x_all = moe_ffn(jnp.concatenate([xp, xs], axis=0), norm_ffn[l], router_w[i], moe_w_gate[i],
                            moe_w_up[i], moe_w_down[i])
            xp, xs = x_all[:mp], x_all[mp:]
    y_prompt = rmsnorm(xp, norm_final, F32, tmp).reshape(BATCH, SEQ, D_MODEL)
    y_sample = rmsnorm(xs, norm_final, F32, ms).reshape(DEC_BATCH, DEC_SEQ, D_MODEL)
    p_out = [jnp.stack(z) for z in zip(*p_states)]
    s_new = [jnp.stack(z) for z in zip(*s_states)]
    s_out = [jnp.concatenate([old, new], axis=2)[:, :, DEC_SEQ:] for old, new in zip(caches[:-1], s_new[:-1])]
    s_out.append(s_new[-1])
    return (y_prompt, y_sample, *p_out, *s_out)
```
